```python
import jax, jax.numpy as jnp
from jax import lax
import numpy as np

D_MODEL = 1024
BATCH = 8
SEQ = 2048
DEPTH = 4
DEC_BATCH = 128
DEC_SEQ = 1
PAST_LEN = 8192
PAGE_SIZE = 128

N_A_LAYERS = DEPTH // 2
N_B_LAYERS = DEPTH - N_A_LAYERS
CHUNK = 128
D_GATE = 2 * D_MODEL
N_GROUPS_A = 8
GROUP_DIM = D_GATE // N_GROUPS_A
D_FF = 4 * D_MODEL
N_HEADS_B = 16
QK_NOPE = 128
QK_ROPE = 64
V_HEAD = 128
Q_LORA = 256
KV_LORA = 512
ROPE_BASE = 10000.0
Q_BLOCK = 128
EPS = 1e-6
SM_SCALE = (QK_NOPE + QK_ROPE) ** -0.5

kernel_name = 'yoco_gmlp_mla_decoder_step'


def rms_norm(x, g):
    xf = x.astype(jnp.float32)
    y = xf * lax.rsqrt(jnp.mean(xf * xf, axis=-1, keepdims=True) + EPS)
    return (y * g.astype(jnp.float32)).astype(x.dtype)


def layer_norm(x, g, b):
    xf = x.astype(jnp.float32)
    mu = jnp.mean(xf, axis=-1, keepdims=True)
    xc = xf - mu
    y = xc * lax.rsqrt(jnp.mean(xc * xc, axis=-1, keepdims=True) + EPS)
    return (y * g.astype(jnp.float32) + b.astype(jnp.float32)).astype(x.dtype)


def rope(x, pos):
    half = QK_ROPE // 2
    inv = ROPE_BASE ** (-jnp.arange(half, dtype=jnp.float32) / half)
    ang = pos.astype(jnp.float32)[:, None] * inv[None, :]
    cos = jnp.cos(ang)[None, :, None, :]
    sin = jnp.sin(ang)[None, :, None, :]
    xf = x.astype(jnp.float32)
    x1, x2 = xf[..., :half], xf[..., half:]
    return jnp.concatenate([x1 * cos - x2 * sin, x1 * sin + x2 * cos], axis=-1).astype(x.dtype)


def chunk_gating_mixer(h, w_in, ln_g, ln_b, w_s, b_s, w_out):
    B, S, _ = h.shape
    z = jax.nn.gelu(h @ w_in, approximate=False)
    u, v = z[..., :D_GATE], z[..., D_GATE:]
    v = layer_norm(v, ln_g, ln_b)
    n_chunks = -(-S // CHUNK)
    pad = n_chunks * CHUNK - S
    vp = jnp.pad(v, ((0, 0), (0, pad), (0, 0))).reshape(B, n_chunks, CHUNK, N_GROUPS_A, GROUP_DIM)
    causal = jnp.tril(jnp.ones((CHUNK, CHUNK), dtype=bool))
    w_causal = jnp.where(causal[None], w_s, jnp.zeros_like(w_s))
    s = jnp.einsum('gij,bnjgc->bnigc', w_causal, vp) + b_s.T[None, None, :, :, None]
    s = s.reshape(B, n_chunks * CHUNK, D_GATE)[:, :S]
    return (u * s) @ w_out, v


def squared_relu_mlp(h, w_up, w_down):
    return jnp.square(jax.nn.relu(h @ w_up)) @ w_down


def shared_latent_kv(x, g_in, w_dkv, g_latent, pos):
    h = rms_norm(x, g_in)
    ckr = h @ w_dkv
    c = rms_norm(ckr[..., :KV_LORA], g_latent)
    kr = rope(ckr[..., KV_LORA:][:, :, None, :], pos)[:, :, 0, :]
    return c, kr


def mla_queries(h, w_dq, g_q, w_uq, pos):
    B, S, _ = h.shape
    cq = rms_norm(h @ w_dq, g_q)
    q = (cq @ w_uq).reshape(B, S, N_HEADS_B, QK_NOPE + QK_ROPE)
    return q[..., :QK_NOPE], rope(q[..., QK_NOPE:], pos)


def prompt_attention(q_nope, q_rope, k_nope, k_rope, v):
    B, S = q_nope.shape[:2]
    nb = S // Q_BLOCK
    qn = q_nope.reshape(B, nb, Q_BLOCK, N_HEADS_B, QK_NOPE).transpose(1, 0, 2, 3, 4)
    qr = q_rope.reshape(B, nb, Q_BLOCK, N_HEADS_B, QK_ROPE).transpose(1, 0, 2, 3, 4)
    kpos = jnp.arange(S, dtype=jnp.int32)

    def block(args):
        i, qn_b, qr_b = args
        s = (jnp.einsum('bqhn,bkhn->bhqk', qn_b, k_nope)
             + jnp.einsum('bqhr,bkr->bhqk', qr_b, k_rope)).astype(jnp.float32) * SM_SCALE
        qpos = i * Q_BLOCK + jnp.arange(Q_BLOCK, dtype=jnp.int32)
        s = jnp.where(kpos[None, :] <= qpos[:, None], s, -jnp.inf)
        p = jax.nn.softmax(s, axis=-1).astype(v.dtype)
        return jnp.einsum('bhqk,bkhv->bqhv', p, v)

    o = lax.map(block, (jnp.arange(nb, dtype=jnp.int32), qn, qr))
    return o.transpose(1, 0, 2, 3, 4).reshape(B, S, N_HEADS_B * V_HEAD)


def sample_attention(q_nope, q_rope, c_past, kr_past, c_new, kr_new, w_uk, w_uv):
    B, Q = q_nope.shape[:2]
    P = c_past.shape[1]
    q_lat = jnp.einsum('bqhn,chn->bqhc', q_nope, w_uk)
    s_past = (jnp.einsum('bqhc,btc->bhqt', q_lat, c_past)
              + jnp.einsum('bqhr,btr->bhqt', q_rope, kr_past)).astype(jnp.float32) * SM_SCALE
    s_new = (jnp.einsum('bqhc,btc->bhqt', q_lat, c_new)
             + jnp.einsum('bqhr,btr->bhqt', q_rope, kr_new)).astype(jnp.float32) * SM_SCALE
    causal = jnp.tril(jnp.ones((Q, Q), dtype=bool))
    s_new = jnp.where(causal, s_new, -jnp.inf)
    p = jax.nn.softmax(jnp.concatenate([s_past, s_new], axis=-1), axis=-1).astype(c_past.dtype)
    o_lat = (jnp.einsum('bhqt,btc->bqhc', p[..., :P], c_past)
             + jnp.einsum('bhqt,btc->bqhc', p[..., P:], c_new))
    o = jnp.einsum('bqhc,chv->bqhv', o_lat, w_uv)
    return o.reshape(B, Q, N_HEADS_B * V_HEAD)


def setup_inputs(seed: int = 0) -> dict:
    key = jax.random.key(seed)
    ks = iter(jax.random.split(key, 40))

    def nrm(shape, scale):
        return scale * jax.random.normal(next(ks), shape, jnp.float32)

    def gain(shape):
        return 1.0 + nrm(shape, 0.02)

    n_pages = PAST_LEN // PAGE_SIZE
    n_used = DEC_BATCH * n_pages
    n_pool = n_used + n_used // 4
    x_prompt = nrm((BATCH, SEQ, D_MODEL), 1.0)
    x_sample = nrm((DEC_BATCH, DEC_SEQ, D_MODEL), 1.0)
    cache_kv_latent = nrm((n_pool, PAGE_SIZE, KV_LORA), 1.0)
    cache_k_rope = nrm((n_pool, PAGE_SIZE, QK_ROPE), 1.0)
    page_table = jax.random.permutation(next(ks), n_pool)[:n_used].reshape(DEC_BATCH, n_pages).astype(jnp.int32)
    return {
        'x_prompt': x_prompt,
        'x_sample': x_sample,
        'cache_kv_latent': cache_kv_latent,
        'cache_k_rope': cache_k_rope,
        'page_table': page_table,
        'norm_pre_mix': gain((DEPTH, D_MODEL)),
        'norm_post_mix': gain((DEPTH, D_MODEL)),
        'norm_pre_ffn': gain((DEPTH, D_MODEL)),
        'norm_post_ffn': gain((DEPTH, D_MODEL)),
        'a_w_in': nrm((N_A_LAYERS, D_MODEL, 2 * D_GATE), D_MODEL ** -0.5),
        'a_ln_g': gain((N_A_LAYERS, D_GATE)),
        'a_ln_b': nrm((N_A_LAYERS, D_GATE), 0.02),
        'a_w_s': nrm((N_A_LAYERS, N_GROUPS_A, CHUNK, CHUNK), CHUNK ** -0.5),
        'a_b_s': 1.0 + nrm((N_A_LAYERS, N_GROUPS_A, CHUNK), 0.1),
        'a_w_out': nrm((N_A_LAYERS, D_GATE, D_MODEL), D_GATE ** -0.5),
        'kv_norm_in': gain((D_MODEL,)),
        'w_dkv': nrm((D_MODEL, KV_LORA + QK_ROPE), D_MODEL ** -0.5),
        'kv_latent_norm': gain((KV_LORA,)),
        'w_uk': nrm((KV_LORA, N_HEADS_B, QK_NOPE), KV_LORA ** -0.5),
        'w_uv': nrm((KV_LORA, N_HEADS_B, V_HEAD), KV_LORA ** -0.5),
        'b_w_dq': nrm((N_B_LAYERS, D_MODEL, Q_LORA), D_MODEL ** -0.5),
        'b_q_norm': gain((N_B_LAYERS, Q_LORA)),
        'b_w_uq': nrm((N_B_LAYERS, Q_LORA, N_HEADS_B * (QK_NOPE + QK_ROPE)), Q_LORA ** -0.5),
        'b_w_o': nrm((N_B_LAYERS, N_HEADS_B * V_HEAD, D_MODEL), (N_HEADS_B * V_HEAD) ** -0.5),
        'ffn_w_up': nrm((DEPTH, D_MODEL, D_FF), D_MODEL ** -0.5),
        'ffn_w_down': nrm((DEPTH, D_FF, D_MODEL), D_FF ** -0.5),
    }


def reference(x_prompt, x_sample, cache_kv_latent, cache_k_rope, page_table,
              norm_pre_mix, norm_post_mix, norm_pre_ffn, norm_post_ffn,
              a_w_in, a_ln_g, a_ln_b, a_w_s, a_b_s, a_w_out,
              kv_norm_in, w_dkv, kv_latent_norm, w_uk, w_uv,
              b_w_dq, b_q_norm, b_w_uq, b_w_o,
              ffn_w_up, ffn_w_down):
    def run_trunk(x, pos, make_attend):
        v_rows = []
        c = kr = None
        attend = None
        for l in range(DEPTH):
            h = rms_norm(x, norm_pre_mix[l])
            if l < N_A_LAYERS:
                m, v = chunk_gating_mixer(h, a_w_in[l], a_ln_g[l], a_ln_b[l], a_w_s[l], a_b_s[l], a_w_out[l])
                v_rows.append(v)
            else:
                j = l - N_A_LAYERS
                if j == 0:
                    c, kr = shared_latent_kv(x, kv_norm_in, w_dkv, kv_latent_norm, pos)
                    attend = make_attend(c, kr)
                q_nope, q_rope = mla_queries(h, b_w_dq[j], b_q_norm[j], b_w_uq[j], pos)
                m = attend(q_nope, q_rope) @ b_w_o[j]
            x = x + rms_norm(m, norm_post_mix[l])
            h = rms_norm(x, norm_pre_ffn[l])
            x = x + rms_norm(squared_relu_mlp(h, ffn_w_up[l], ffn_w_down[l]), norm_post_ffn[l])
        return x, c, kr, v_rows

    def prompt_attend_factory(c, kr):
        k_nope = jnp.einsum('bsc,chn->bshn', c, w_uk)
        v = jnp.einsum('bsc,chv->bshv', c, w_uv)
        return lambda qn, qr: prompt_attention(qn, qr, k_nope, kr, v)

    pos_p = jnp.arange(x_prompt.shape[1], dtype=jnp.int32)
    y_prompt, c_p, kr_p, _ = run_trunk(x_prompt, pos_p, prompt_attend_factory)

    db, n_pages = page_table.shape
    n_past = n_pages * cache_kv_latent.shape[1]
    c_past = cache_kv_latent[page_table].reshape(db, n_past, KV_LORA)
    kr_past = cache_k_rope[page_table].reshape(db, n_past, QK_ROPE)

    def sample_attend_factory(c, kr):
        return lambda qn, qr: sample_attention(qn, qr, c_past, kr_past, c, kr, w_uk, w_uv)

    pos_s = n_past + jnp.arange(x_sample.shape[1], dtype=jnp.int32)
    y_sample, c_s, kr_s, v_rows_s = run_trunk(x_sample, pos_s, sample_attend_factory)
    gate_v_sample = jnp.stack(v_rows_s, axis=0)
    return (y_prompt, y_sample, c_p, kr_p, c_s, kr_s, gate_v_sample)
```

```python
import functools

import jax
import jax.numpy as jnp
from jax import lax
from jax.experimental import pallas as pl
from jax.experimental.pallas import tpu as pltpu

D_MODEL = 1024
CHUNK = 128
D_GATE = 2 * D_MODEL
N_GROUPS = 8
GROUP_DIM = D_GATE // N_GROUPS
D_FF = 4 * D_MODEL
N_HEADS = 16
QK_NOPE = 128
QK_ROPE = 64
QK_DIM = QK_NOPE + QK_ROPE
V_HEAD = 128
Q_LORA = 256
KV_LORA = 512
LAT_DIM = KV_LORA + QK_ROPE
ROPE_BASE = 10000.0
EPS = 1e-6
SM_SCALE = QK_DIM ** -0.5
N_A_LAYERS = 2
N_B_LAYERS = 2

F32 = jnp.float32
BF16 = jnp.bfloat16

VMEM_LIMIT_BYTES = 56 * 1024 * 1024

PROMPT_TOKEN_TILE = 256
ATTN_Q_TILE = 256
ATTN_KV_TILE = 256
PAGES_PER_STEP = 8


def _dot(a, b):
    return jnp.dot(a, b, preferred_element_type=F32)


def _dot_nt(a, b):
    return lax.dot_general(a, b, (((1,), (1,)), ((), ())), preferred_element_type=F32)


def _rms(x, g):
    return x * lax.rsqrt(jnp.mean(x * x, axis=-1, keepdims=True) + EPS) * g


def _gelu(z):
    return 0.5 * z * (1.0 + lax.erf(z * (0.5 ** 0.5)))


def _const_spec(shape):
    n = len(shape)
    return pl.BlockSpec(shape, lambda *_: (0,) * n, pipeline_mode=pl.Buffered(1))


def _params(n_grid):
    return pltpu.CompilerParams(
        dimension_semantics=("arbitrary",) * n_grid,
        vmem_limit_bytes=VMEM_LIMIT_BYTES,
    )


def _mixer_kernel(x_ref, gpre_ref, win_ref, lng_ref, lnb_ref, ws_ref, bias_ref, wout_ref,
                  gpost_ref, *rest, tm, single):
    if single:
        o_ref, v_ref, gated_ref = rest
    else:
        o_ref, gated_ref = rest
    x = x_ref[...]
    h = _rms(x, gpre_ref[...]).astype(BF16)
    z = _gelu(_dot(h, win_ref[...]))
    u = z[:, :D_GATE]
    v = z[:, D_GATE:]
    mu = jnp.mean(v, axis=-1, keepdims=True)
    vc = v - mu
    v = vc * lax.rsqrt(jnp.mean(vc * vc, axis=-1, keepdims=True) + EPS) * lng_ref[...] + lnb_ref[...]
    if single:
        v_ref[...] = v
        gated_ref[...] = (u * (v * ws_ref[...] + bias_ref[...])).astype(BF16)
    else:
        vb = v.astype(BF16)
        row = lax.broadcasted_iota(jnp.int32, (CHUNK, CHUNK), 0)
        col = lax.broadcasted_iota(jnp.int32, (CHUNK, CHUNK), 1)
        causal = row >= col
        for g in range(N_GROUPS):
            wg = jnp.where(causal, ws_ref[g], 0.0).astype(BF16)
            cols = slice(g * GROUP_DIM, (g + 1) * GROUP_DIM)
            for c in range(tm // CHUNK):
                rows = slice(c * CHUNK, (c + 1) * CHUNK)
                s = _dot(wg, vb[rows, cols]) + bias_ref[:, cols]
                gated_ref[rows, cols] = (u[rows, cols] * s).astype(BF16)
    m = _dot(gated_ref[...], wout_ref[...])
    o_ref[...] = x + _rms(m, gpost_ref[...])


def _mixer_call(x, gpre, win, lng, lnb, ws, bias, wout, gpost, *, tm, single):
    t = x.shape[0]
    tok = lambda w: pl.BlockSpec((tm, w), lambda i: (i, 0))
    out_shape = [jax.ShapeDtypeStruct((t, D_MODEL), F32)]
    out_specs = [tok(D_MODEL)]
    if single:
        out_shape.append(jax.ShapeDtypeStruct((t, D_GATE), F32))
        out_specs.append(tok(D_GATE))
    return pl.pallas_call(
        functools.partial(_mixer_kernel, tm=tm, single=single),
        grid=(t // tm,),
        in_specs=[tok(D_MODEL), _const_spec(gpre.shape), _const_spec(win.shape),
                  _const_spec(lng.shape), _const_spec(lnb.shape), _const_spec(ws.shape),
                  _const_spec(bias.shape), _const_spec(wout.shape), _const_spec(gpost.shape)],
        out_specs=out_specs,
        out_shape=out_shape,
        scratch_shapes=[pltpu.VMEM((tm, D_GATE), BF16)],
        compiler_params=_params(1),
        name="mixer_single" if single else "mixer_chunk",
    )(x, gpre, win, lng, lnb, ws, bias, wout, gpost)


def _tail_kernel(*refs, has_oproj):
    if has_oproj:
        x_ref, a_ref, wo_ref, gpm_ref, gpf_ref, wup_ref, wdn_ref, gpo_ref, o_ref = refs
    else:
        x_ref, gpf_ref, wup_ref, wdn_ref, gpo_ref, o_ref = refs
    x = x_ref[...]
    if has_oproj:
        x = x + _rms(_dot(a_ref[...], wo_ref[...]), gpm_ref[...])
    h = _rms(x, gpf_ref[...]).astype(BF16)
    a = jnp.square(jnp.maximum(_dot(h, wup_ref[...]), 0.0)).astype(BF16)
    o_ref[...] = x + _rms(_dot(a, wdn_ref[...]), gpo_ref[...])


def _tail_call(x, attn, wo, gpm, gpf, wup, wdn, gpo, *, tm):
    t = x.shape[0]
    tok = lambda w: pl.BlockSpec((tm, w), lambda i: (i, 0))
    has_oproj = attn is not None
    if has_oproj:
        args = (x, attn, wo, gpm, gpf, wup, wdn, gpo)
        in_specs = [tok(D_MODEL), tok(attn.shape[1])] + [_const_spec(a.shape) for a in args[2:]]
    else:
        args = (x, gpf, wup, wdn, gpo)
        in_specs = [tok(D_MODEL)] + [_const_spec(a.shape) for a in args[1:]]
    return pl.pallas_call(
        functools.partial(_tail_kernel, has_oproj=has_oproj),
        grid=(t // tm,),
        in_specs=in_specs,
        out_specs=tok(D_MODEL),
        out_shape=jax.ShapeDtypeStruct((t, D_MODEL), F32),
        compiler_params=_params(1),
        name="tail_oproj" if has_oproj else "tail_ffn",
    )(*args)


def _latent_kernel(*refs, expand):
    if expand:
        (x_ref, gin_ref, wdkv_ref, glat_ref, cos_ref, sin_ref, wuk_ref, wuv_ref,
         c_ref, kr_ref, kfull_ref, v_ref) = refs
    else:
        x_ref, gin_ref, wdkv_ref, glat_ref, cos_ref, sin_ref, c_ref, kr_ref = refs
    h = _rms(x_ref[...], gin_ref[...]).astype(BF16)
    ckr = _dot(h, wdkv_ref[...])
    c = _rms(ckr[:, :KV_LORA], glat_ref[...])
    k = ckr[:, KV_LORA:]
    half = QK_ROPE // 2
    swapped = jnp.concatenate([k[:, half:], k[:, :half]], axis=1)
    kr = k * cos_ref[...] + swapped * sin_ref[...]
    c_ref[...] = c
    kr_ref[...] = kr
    if expand:
        cb = c.astype(BF16)
        krb = kr.astype(BF16)
        kn = _dot(cb, wuk_ref[...])
        vv = _dot(cb, wuv_ref[...])
        for hd in range(N_HEADS):
            kfull_ref[0, hd, :, 0:QK_NOPE] = kn[:, hd * QK_NOPE:(hd + 1) * QK_NOPE].astype(BF16)
            kfull_ref[0, hd, :, QK_NOPE:QK_DIM] = krb
            v_ref[0, hd] = vv[:, hd * V_HEAD:(hd + 1) * V_HEAD].astype(BF16)


def _latent_call(x, gin, wdkv, glat, cos, sin, wuk, wuv, *, tm, seq, expand):
    t = x.shape[0]
    per_seq = seq // tm
    tok = lambda w: pl.BlockSpec((tm, w), lambda i: (i, 0))
    pos = lambda w: pl.BlockSpec((tm, w), lambda i: (i % per_seq, 0))
    args = [x, gin, wdkv, glat, cos, sin]
    in_specs = [tok(D_MODEL), _const_spec(gin.shape), _const_spec(wdkv.shape),
                _const_spec(glat.shape), pos(QK_ROPE), pos(QK_ROPE)]
    out_shape = [jax.ShapeDtypeStruct((t, KV_LORA), F32), jax.ShapeDtypeStruct((t, QK_ROPE), F32)]
    out_specs = [tok(KV_LORA), tok(QK_ROPE)]
    if expand:
        args += [wuk, wuv]
        in_specs += [_const_spec(wuk.shape), _const_spec(wuv.shape)]
        nb = t // seq
        head_major = lambda w: pl.BlockSpec(
            (1, N_HEADS, tm, w), lambda i: (i // per_seq, 0, i % per_seq, 0))
        out_shape += [jax.ShapeDtypeStruct((nb, N_HEADS, seq, QK_DIM), BF16),
                      jax.ShapeDtypeStruct((nb, N_HEADS, seq, V_HEAD), BF16)]
        out_specs += [head_major(QK_DIM), head_major(V_HEAD)]
    return pl.pallas_call(
        functools.partial(_latent_kernel, expand=expand),
        grid=(t // tm,),
        in_specs=in_specs,
        out_specs=out_specs,
        out_shape=out_shape,
        compiler_params=_params(1),
        name="latent_expand" if expand else "latent",
    )(*args)


def _q_kernel(*refs, absorb):
    if absorb:
        x_ref, gpre_ref, wdq_ref, gq_ref, wuq_ref, cos_ref, sin_ref, wukt_ref, q_ref = refs
    else:
        x_ref, gpre_ref, wdq_ref, gq_ref, wuq_ref, cos_ref, sin_ref, q_ref = refs
    h = _rms(x_ref[...], gpre_ref[...]).astype(BF16)
    cq = _rms(_dot(h, wdq_ref[...]), gq_ref[...]).astype(BF16)
    q = _dot(cq, wuq_ref[...])
    qn = q[:, :N_HEADS * QK_NOPE]
    qr = q[:, N_HEADS * QK_NOPE:]
    width = N_HEADS * QK_ROPE
    half = QK_ROPE // 2
    lane = lax.broadcasted_iota(jnp.int32, qr.shape, 1)
    first_half = (lane & (QK_ROPE - 1)) < half
    swapped = jnp.where(first_half, pltpu.roll(qr, width - half, 1), pltpu.roll(qr, half, 1))
    qr = qr * cos_ref[...] + swapped * sin_ref[...]
    for hd in range(N_HEADS):
        qn_h = qn[:, hd * QK_NOPE:(hd + 1) * QK_NOPE]
        qr_h = (qr[:, hd * QK_ROPE:(hd + 1) * QK_ROPE] * SM_SCALE).astype(BF16)
        if absorb:
            q_lat = _dot(qn_h.astype(BF16), wukt_ref[hd])
            q_ref[hd, :, 0:KV_LORA] = (q_lat * SM_SCALE).astype(BF16)
            q_ref[hd, :, KV_LORA:LAT_DIM] = qr_h
        else:
            q_ref[0, hd, :, 0:QK_NOPE] = (qn_h * SM_SCALE).astype(BF16)
            q_ref[0, hd, :, QK_NOPE:QK_DIM] = qr_h


def _q_call(x, gpre, wdq, gq, wuq, cos, sin, wukt, *, tm, seq, absorb):
    t = x.shape[0]
    per_seq = seq // tm
    width = N_HEADS * QK_ROPE
    tok = lambda w: pl.BlockSpec((tm, w), lambda i: (i, 0))
    pos = lambda w: pl.BlockSpec((tm, w), lambda i: (i % per_seq, 0))
    args = [x, gpre, wdq, gq, wuq, cos, sin]
    in_specs = [tok(D_MODEL), _const_spec(gpre.shape), _const_spec(wdq.shape),
                _const_spec(gq.shape), _const_spec(wuq.shape), pos(width), pos(width)]
    if absorb:
        args.append(wukt)
        in_specs.append(_const_spec(wukt.shape))
        out_shape = jax.ShapeDtypeStruct((N_HEADS, t, LAT_DIM), BF16)
        out_spec = pl.BlockSpec((N_HEADS, tm, LAT_DIM), lambda i: (0, i, 0))
    else:
        out_shape = jax.ShapeDtypeStruct((t // seq, N_HEADS, seq, QK_DIM), BF16)
        out_spec = pl.BlockSpec((1, N_HEADS, tm, QK_DIM),
                                lambda i: (i // per_seq, 0, i % per_seq, 0))
    return pl.pallas_call(
        functools.partial(_q_kernel, absorb=absorb),
        grid=(t // tm,),
        in_specs=in_specs,
        out_specs=out_spec,
        out_shape=out_shape,
        compiler_params=_params(1),
        name="q_absorb" if absorb else "q_heads",
    )(*args)


def _prompt_attn_kernel(q_ref, k_ref, v_ref, o_ref, *, tq, tk):
    i = pl.program_id(2)
    q = q_ref[0, 0]
    n_full = (i * tq) // tk
    diag_tiles = tq // tk

    def step(start, carry, mask):
        m, l, acc = carry
        k = k_ref[0, 0, pl.ds(start, tk), :]
        v = v_ref[0, 0, pl.ds(start, tk), :]
        s = _dot_nt(q, k)
        if mask is not None:
            s = jnp.where(mask, s, -jnp.inf)
        m_new = jnp.maximum(m, jnp.max(s, axis=-1, keepdims=True))
        alpha = jnp.exp(m - m_new)
        p = jnp.exp(s - m_new)
        l = alpha * l + jnp.sum(p, axis=-1, keepdims=True)
        acc = alpha * acc + _dot(p.astype(BF16), v)
        return m_new, l, acc

    init = (jnp.full((tq, 1), -jnp.inf, F32), jnp.zeros((tq, 1), F32),
            jnp.zeros((tq, V_HEAD), F32))
    carry = lax.fori_loop(
        0, n_full, lambda j, c: step(pl.multiple_of(j * tk, tk), c, None), init)
    qpos = lax.broadcasted_iota(jnp.int32, (tq, tk), 0)
    kpos = lax.broadcasted_iota(jnp.int32, (tq, tk), 1)
    for d in range(diag_tiles):
        start = pl.multiple_of(i * tq + d * tk, tk)
        carry = step(start, carry, kpos + d * tk <= qpos)
    _, l, acc = carry
    o_ref[0] = (acc / l).astype(o_ref.dtype)


def _prompt_attn_call(q, k, v, *, tq, tk):
    nb, nh, seq, _ = q.shape
    return pl.pallas_call(
        functools.partial(_prompt_attn_kernel, tq=tq, tk=tk),
        grid=(nb, nh, seq // tq),
        in_specs=[pl.BlockSpec((1, 1, tq, QK_DIM), lambda b, h, i: (b, h, i, 0)),
                  pl.BlockSpec((1, 1, seq, QK_DIM), lambda b, h, i: (b, h, 0, 0)),
                  pl.BlockSpec((1, 1, seq, V_HEAD), lambda b, h, i: (b, h, 0, 0))],
        out_specs=pl.BlockSpec((1, tq, V_HEAD), lambda b, h, i: (b, i, h)),
        out_shape=jax.ShapeDtypeStruct((nb, seq, nh * V_HEAD), BF16),
        compiler_params=_params(3),
        name="prompt_attn",
    )(q, k, v)


def _sample_attn_kernel(pt_ref, q_ref, cnew_ref, krnew_ref, *rest, pps):
    del pt_ref
    c_refs = rest[:pps]
    kr_refs = rest[pps:2 * pps]
    o_ref, kv_ref, m_ref, l_ref, acc_ref = rest[2 * pps:]
    j = pl.program_id(1)
    q = q_ref[0]
    page = c_refs[0].shape[1]

    @pl.when(j == 0)
    def _():
        qf = q.astype(F32)
        c_new = cnew_ref[0]
        s_new = (jnp.sum(qf[:, :KV_LORA] * c_new, axis=-1, keepdims=True)
                 + jnp.sum(qf[:, KV_LORA:] * krnew_ref[0], axis=-1, keepdims=True))
        m_ref[...] = s_new
        l_ref[...] = jnp.ones_like(l_ref)
        acc_ref[...] = jnp.broadcast_to(c_new, acc_ref.shape)

    for r in range(pps):
        kv_ref[r * page:(r + 1) * page, 0:KV_LORA] = c_refs[r][0].astype(BF16)
        kv_ref[r * page:(r + 1) * page, KV_LORA:LAT_DIM] = kr_refs[r][0].astype(BF16)
    kv = kv_ref[...]
    s = _dot_nt(q, kv)
    m = m_ref[...]
    m_new = jnp.maximum(m, jnp.max(s, axis=-1, keepdims=True))
    alpha = jnp.exp(m - m_new)
    p = jnp.exp(s - m_new)
    l_ref[...] = alpha * l_ref[...] + jnp.sum(p, axis=-1, keepdims=True)
    acc_ref[...] = alpha * acc_ref[...] + _dot(p.astype(BF16), kv[:, :KV_LORA])
    m_ref[...] = m_new

    @pl.when(j == pl.num_programs(1) - 1)
    def _():
        o_ref[0] = (acc_ref[...] / l_ref[...]).astype(o_ref.dtype)


def _sample_attn_call(page_table, q, c_new, kr_new, cache_c, cache_kr, *, pps):
    nb, n_pages = page_table.shape
    page = cache_c.shape[1]
    pt_flat = page_table.reshape(-1)

    def page_spec(width, r):
        return pl.BlockSpec(
            (1, page, width), lambda b, j, pt: (pt[b * n_pages + j * pps + r], 0, 0))

    row = lambda w: pl.BlockSpec((1, 1, w), lambda b, j, pt: (b, 0, 0))
    in_specs = ([pl.BlockSpec((1, N_HEADS, LAT_DIM), lambda b, j, pt: (b, 0, 0)),
                 row(KV_LORA), row(QK_ROPE)]
                + [page_spec(KV_LORA, r) for r in range(pps)]
                + [page_spec(QK_ROPE, r) for r in range(pps)])
    return pl.pallas_call(
        functools.partial(_sample_attn_kernel, pps=pps),
        grid_spec=pltpu.PrefetchScalarGridSpec(
            num_scalar_prefetch=1,
            grid=(nb, n_pages // pps),
            in_specs=in_specs,
            out_specs=pl.BlockSpec((1, N_HEADS, KV_LORA), lambda b, j, pt: (b, 0, 0)),
            scratch_shapes=[pltpu.VMEM((pps * page, LAT_DIM), BF16),
                            pltpu.VMEM((N_HEADS, 1), F32),
                            pltpu.VMEM((N_HEADS, 1), F32),
                            pltpu.VMEM((N_HEADS, KV_LORA), F32)],
        ),
        out_shape=jax.ShapeDtypeStruct((nb, N_HEADS, KV_LORA), BF16),
        compiler_params=_params(2),
        name="sample_attn",
    )(pt_flat, q, c_new.reshape(nb, 1, KV_LORA), kr_new.reshape(nb, 1, QK_ROPE),
      *([cache_c] * pps), *([cache_kr] * pps))


def _uv_kernel(o_ref, wuv_ref, out_ref):
    for hd in range(N_HEADS):
        out_ref[:, hd * V_HEAD:(hd + 1) * V_HEAD] = _dot(o_ref[hd], wuv_ref[hd]).astype(out_ref.dtype)


def _uv_call(o_lat, wuv_heads):
    nh, t, _ = o_lat.shape
    return pl.pallas_call(
        _uv_kernel,
        out_shape=jax.ShapeDtypeStruct((t, nh * V_HEAD), BF16),
        compiler_params=pltpu.CompilerParams(vmem_limit_bytes=VMEM_LIMIT_BYTES),
        name="uv_proj",
    )(o_lat, wuv_heads)


def _rope_tables(pos, reps):
    half = QK_ROPE // 2
    inv = ROPE_BASE ** (-jnp.arange(half, dtype=F32) / half)
    ang = pos.astype(F32)[:, None] * inv[None, :]
    cos = jnp.cos(ang)
    sin = jnp.sin(ang)
    cos64 = jnp.concatenate([cos, cos], axis=1)
    sin64 = jnp.concatenate([-sin, sin], axis=1)
    return cos64, sin64, jnp.tile(cos64, (1, reps)), jnp.tile(sin64, (1, reps))


def _row(v):
    return v.reshape(1, -1).astype(F32)


@jax.jit
def _forward(x_prompt, x_sample, cache_kv_latent, cache_k_rope, page_table,
             norm_pre_mix, norm_post_mix, norm_pre_ffn, norm_post_ffn,
             a_w_in, a_ln_g, a_ln_b, a_w_s, a_b_s, a_w_out,
             kv_norm_in, w_dkv, kv_latent_norm, w_uk, w_uv,
             b_w_dq, b_q_norm, b_w_uq, b_w_o, ffn_w_up, ffn_w_down):
    nb, seq, _ = x_prompt.shape
    db, dseq, _ = x_sample.shape
    n_pages = page_table.shape[1]
    n_past = n_pages * cache_kv_latent.shape[1]

    bf = lambda w: w.astype(BF16)
    w_in = bf(a_w_in)
    w_out = bf(a_w_out)
    w_up = bf(ffn_w_up)
    w_down = bf(ffn_w_down)
    w_dkv_b = bf(w_dkv)
    w_uk2d = bf(w_uk.reshape(KV_LORA, N_HEADS * QK_NOPE))
    w_uv2d = bf(w_uv.reshape(KV_LORA, N_HEADS * V_HEAD))
    w_ukt = bf(jnp.transpose(w_uk, (1, 2, 0)))
    w_uv_heads = bf(jnp.transpose(w_uv, (1, 0, 2)))
    w_dq = bf(b_w_dq)
    w_o = bf(b_w_o)
    w_uq4 = b_w_uq.reshape(N_B_LAYERS, Q_LORA, N_HEADS, QK_DIM)
    w_uq = bf(jnp.concatenate(
        [w_uq4[..., :QK_NOPE].reshape(N_B_LAYERS, Q_LORA, N_HEADS * QK_NOPE),
         w_uq4[..., QK_NOPE:].reshape(N_B_LAYERS, Q_LORA, N_HEADS * QK_ROPE)], axis=-1))

    bias_chunk = jnp.repeat(jnp.transpose(a_b_s, (0, 2, 1)), GROUP_DIM, axis=2)
    bias_single = jnp.repeat(a_b_s[:, :, 0], GROUP_DIM, axis=1)[:, None, :]
    w_single = jnp.repeat(a_w_s[:, :, 0, 0], GROUP_DIM, axis=1)[:, None, :]

    def trunk(x, pos, *, tm, seq_len, single):
        cos64, sin64, cos_w, sin_w = _rope_tables(pos, N_HEADS)
        v_rows = []
        c = kr = kfull = vfull = None
        for l in range(N_A_LAYERS + N_B_LAYERS):
            if l < N_A_LAYERS:
                outs = _mixer_call(
                    x, _row(norm_pre_mix[l]), w_in[l], _row(a_ln_g[l]), _row(a_ln_b[l]),
                    w_single[l] if single else a_w_s[l],
                    bias_single[l] if single else bias_chunk[l],
                    w_out[l], _row(norm_post_mix[l]), tm=tm, single=single)
                x = outs[0]
                if single:
                    v_rows.append(outs[1])
                attn = None
            else:
                j = l - N_A_LAYERS
                if j == 0:
                    outs = _latent_call(
                        x, _row(kv_norm_in), w_dkv_b, _row(kv_latent_norm), cos64, sin64,
                        w_uk2d, w_uv2d, tm=tm, seq=seq_len, expand=not single)
                    c, kr = outs[0], outs[1]
                    if not single:
                        kfull, vfull = outs[2], outs[3]
                q = _q_call(x, _row(norm_pre_mix[l]), w_dq[j], _row(b_q_norm[j]), w_uq[j],
                            cos_w, sin_w, w_ukt, tm=tm, seq=seq_len, absorb=single)
                if single:
                    o_lat = _sample_attn_call(
                        page_table, jnp.transpose(q, (1, 0, 2)), c, kr,
                        cache_kv_latent, cache_k_rope, pps=PAGES_PER_STEP)
                    attn = _uv_call(jnp.transpose(o_lat, (1, 0, 2)), w_uv_heads)
                else:
                    attn = _prompt_attn_call(q, kfull, vfull, tq=ATTN_Q_TILE, tk=ATTN_KV_TILE)
                    attn = attn.reshape(x.shape[0], N_HEADS * V_HEAD)
            x = _tail_call(
                x, attn, None if attn is None else w_o[l - N_A_LAYERS],
                None if attn is None else _row(norm_post_mix[l]),
                _row(norm_pre_ffn[l]), w_up[l], w_down[l], _row(norm_post_ffn[l]), tm=tm)
        return x, c, kr, v_rows

    pos_p = jnp.arange(seq, dtype=jnp.int32)
    y_p, c_p, kr_p, _ = trunk(x_prompt.reshape(nb * seq, D_MODEL), pos_p,
                              tm=PROMPT_TOKEN_TILE, seq_len=seq, single=False)
    pos_s = jnp.full((db * dseq,), n_past, dtype=jnp.int32)
    y_s, c_s, kr_s, v_rows = trunk(x_sample.reshape(db * dseq, D_MODEL), pos_s,
                                   tm=db * dseq, seq_len=db * dseq, single=True)
    gate_v = jnp.stack(v_rows, axis=0).reshape(N_A_LAYERS, db, dseq, D_GATE)
    return (y_p.reshape(nb, seq, D_MODEL), y_s.reshape(db, dseq, D_MODEL),
            c_p.reshape(nb, seq, KV_LORA), kr_p.reshape(nb, seq, QK_ROPE),
            c_s.reshape(db, dseq, KV_LORA), kr_s.reshape(db, dseq, QK_ROPE), gate_v)


def kernel(x_prompt, x_sample, cache_kv_latent, cache_k_rope, page_table, norm_pre_mix, norm_post_mix, norm_pre_ffn, norm_post_ffn, a_w_in, a_ln_g, a_ln_b, a_w_s, a_b_s, a_w_out, kv_norm_in, w_dkv, kv_latent_norm, w_uk, w_uv, b_w_dq, b_q_norm, b_w_uq, b_w_o, ffn_w_up, ffn_w_down):
    assert x_sample.shape[1] == 1, "sample path assumes one new token per row"
    return _forward(x_prompt, x_sample, cache_kv_latent, cache_k_rope, page_table,
                    norm_pre_mix, norm_post_mix, norm_pre_ffn, norm_post_ffn,
                    a_w_in, a_ln_g, a_ln_b, a_w_s, a_b_s, a_w_out,
                    kv_norm_in, w_dkv, kv_latent_norm, w_uk, w_uv,
                    b_w_dq, b_q_norm, b_w_uq, b_w_o, ffn_w_up, ffn_w_down)
```

```python
import functools

import jax
import jax.numpy as jnp
from jax import lax
from jax.experimental import pallas as pl
from jax.experimental.pallas import tpu as pltpu

D_MODEL = 1024
CHUNK = 128
D_GATE = 2 * D_MODEL
N_GROUPS = 8
GROUP_DIM = D_GATE // N_GROUPS
D_FF = 4 * D_MODEL
N_HEADS = 16
QK_NOPE = 128
QK_ROPE = 64
QK_DIM = QK_NOPE + QK_ROPE
V_HEAD = 128
Q_LORA = 256
KV_LORA = 512
LAT_DIM = KV_LORA + QK_ROPE
ROPE_BASE = 10000.0
EPS = 1e-6
SM_SCALE = QK_DIM ** -0.5
N_A_LAYERS = 2
N_B_LAYERS = 2

F32 = jnp.float32
BF16 = jnp.bfloat16

VMEM_LIMIT_BYTES = 56 * 1024 * 1024

LANES = 128
V_EXT = 2 * LANES
LOG2E = 1.4426950408889634

PROMPT_TOKEN_TILE = 256
ATTN_Q_TILE = 512
ATTN_KV_TILE = 512
ATTN_HEADS_PER_STEP = 8
PAGES_PER_CHUNK = 16


def _dot(a, b):
    return jnp.dot(a, b, preferred_element_type=F32)


def _dot_nt(a, b):
    return lax.dot_general(a, b, (((1,), (1,)), ((), ())), preferred_element_type=F32)


def _rms(x, g):
    return x * lax.rsqrt(jnp.mean(x * x, axis=-1, keepdims=True) + EPS) * g


def _gelu(z):
    return 0.5 * z * (1.0 + lax.erf(z * (0.5 ** 0.5)))


def _const_spec(shape):
    n = len(shape)
    return pl.BlockSpec(shape, lambda *_: (0,) * n, pipeline_mode=pl.Buffered(1))


def _params(n_grid):
    return pltpu.CompilerParams(
        dimension_semantics=("arbitrary",) * n_grid,
        vmem_limit_bytes=VMEM_LIMIT_BYTES,
    )


def _mixer_kernel(x_ref, gpre_ref, win_ref, lng_ref, lnb_ref, ws_ref, bias_ref, wout_ref,
                  gpost_ref, *rest, tm, single):
    if single:
        o_ref, v_ref, gated_ref = rest
    else:
        o_ref, gated_ref = rest
    x = x_ref[...]
    h = _rms(x, gpre_ref[...]).astype(BF16)
    z = _gelu(_dot(h, win_ref[...]))
    u = z[:, :D_GATE]
    v = z[:, D_GATE:]
    mu = jnp.mean(v, axis=-1, keepdims=True)
    vc = v - mu
    v = vc * lax.rsqrt(jnp.mean(vc * vc, axis=-1, keepdims=True) + EPS) * lng_ref[...] + lnb_ref[...]
    if single:
        v_ref[...] = v
        gated_ref[...] = (u * (v * ws_ref[...] + bias_ref[...])).astype(BF16)
    else:
        vb = v.astype(BF16)
        row = lax.broadcasted_iota(jnp.int32, (CHUNK, CHUNK), 0)
        col = lax.broadcasted_iota(jnp.int32, (CHUNK, CHUNK), 1)
        causal = row >= col
        for g in range(N_GROUPS):
            wg = jnp.where(causal, ws_ref[g], 0.0).astype(BF16)
            cols = slice(g * GROUP_DIM, (g + 1) * GROUP_DIM)
            for c in range(tm // CHUNK):
                rows = slice(c * CHUNK, (c + 1) * CHUNK)
                s = _dot(wg, vb[rows, cols]) + bias_ref[:, cols]
                gated_ref[rows, cols] = (u[rows, cols] * s).astype(BF16)
    m = _dot(gated_ref[...], wout_ref[...])
    o_ref[...] = x + _rms(m, gpost_ref[...])


def _mixer_call(x, gpre, win, lng, lnb, ws, bias, wout, gpost, *, tm, single):
    t = x.shape[0]
    tok = lambda w: pl.BlockSpec((tm, w), lambda i: (i, 0))
    out_shape = [jax.ShapeDtypeStruct((t, D_MODEL), F32)]
    out_specs = [tok(D_MODEL)]
    if single:
        out_shape.append(jax.ShapeDtypeStruct((t, D_GATE), F32))
        out_specs.append(tok(D_GATE))
    return pl.pallas_call(
        functools.partial(_mixer_kernel, tm=tm, single=single),
        grid=(t // tm,),
        in_specs=[tok(D_MODEL), _const_spec(gpre.shape), _const_spec(win.shape),
                  _const_spec(lng.shape), _const_spec(lnb.shape), _const_spec(ws.shape),
                  _const_spec(bias.shape), _const_spec(wout.shape), _const_spec(gpost.shape)],
        out_specs=out_specs,
        out_shape=out_shape,
        scratch_shapes=[pltpu.VMEM((tm, D_GATE), BF16)],
        compiler_params=_params(1),
        name="mixer_single" if single else "mixer_chunk",
    )(x, gpre, win, lng, lnb, ws, bias, wout, gpost)


def _tail_kernel(*refs, has_oproj):
    if has_oproj:
        x_ref, a_ref, wo_ref, gpm_ref, gpf_ref, wup_ref, wdn_ref, gpo_ref, o_ref = refs
    else:
        x_ref, gpf_ref, wup_ref, wdn_ref, gpo_ref, o_ref = refs
    x = x_ref[...]
    if has_oproj:
        x = x + _rms(_dot(a_ref[...], wo_ref[...]), gpm_ref[...])
    h = _rms(x, gpf_ref[...]).astype(BF16)
    a = jnp.square(jnp.maximum(_dot(h, wup_ref[...]), 0.0)).astype(BF16)
    o_ref[...] = x + _rms(_dot(a, wdn_ref[...]), gpo_ref[...])


def _tail_call(x, attn, wo, gpm, gpf, wup, wdn, gpo, *, tm):
    t = x.shape[0]
    tok = lambda w: pl.BlockSpec((tm, w), lambda i: (i, 0))
    has_oproj = attn is not None
    if has_oproj:
        args = (x, attn, wo, gpm, gpf, wup, wdn, gpo)
        in_specs = [tok(D_MODEL), tok(attn.shape[1])] + [_const_spec(a.shape) for a in args[2:]]
    else:
        args = (x, gpf, wup, wdn, gpo)
        in_specs = [tok(D_MODEL)] + [_const_spec(a.shape) for a in args[1:]]
    return pl.pallas_call(
        functools.partial(_tail_kernel, has_oproj=has_oproj),
        grid=(t // tm,),
        in_specs=in_specs,
        out_specs=tok(D_MODEL),
        out_shape=jax.ShapeDtypeStruct((t, D_MODEL), F32),
        compiler_params=_params(1),
        name="tail_oproj" if has_oproj else "tail_ffn",
    )(*args)


def _latent_kernel(*refs, expand):
    if expand:
        (x_ref, gin_ref, wdkv_ref, glat_ref, cos_ref, sin_ref, wuk_ref, wuv_ref,
         c_ref, kr_ref, kfull_ref, v_ref) = refs
    else:
        x_ref, gin_ref, wdkv_ref, glat_ref, cos_ref, sin_ref, c_ref, kr_ref = refs
    h = _rms(x_ref[...], gin_ref[...]).astype(BF16)
    ckr = _dot(h, wdkv_ref[...])
    c = _rms(ckr[:, :KV_LORA], glat_ref[...])
    k = ckr[:, KV_LORA:]
    half = QK_ROPE // 2
    swapped = jnp.concatenate([k[:, half:], k[:, :half]], axis=1)
    kr = k * cos_ref[...] + swapped * sin_ref[...]
    c_ref[...] = c
    kr_ref[...] = kr
    if expand:
        cb = c.astype(BF16)
        krb = kr.astype(BF16)
        kn = _dot(cb, wuk_ref[...])
        vv = _dot(cb, wuv_ref[...])
        lane = lax.broadcasted_iota(jnp.int32, (cb.shape[0], V_EXT - V_HEAD), 1)
        ones_col = jnp.where(lane == 0, 1.0, 0.0).astype(BF16)
        for hd in range(N_HEADS):
            kfull_ref[0, hd, :, 0:QK_NOPE] = kn[:, hd * QK_NOPE:(hd + 1) * QK_NOPE].astype(BF16)
            kfull_ref[0, hd, :, QK_NOPE:QK_DIM] = krb
            v_ref[0, hd, :, 0:V_HEAD] = vv[:, hd * V_HEAD:(hd + 1) * V_HEAD].astype(BF16)
            v_ref[0, hd, :, V_HEAD:V_EXT] = ones_col


def _latent_call(x, gin, wdkv, glat, cos, sin, wuk, wuv, *, tm, seq, expand):
    t = x.shape[0]
    per_seq = seq // tm
    tok = lambda w: pl.BlockSpec((tm, w), lambda i: (i, 0))
    pos = lambda w: pl.BlockSpec((tm, w), lambda i: (i % per_seq, 0))
    args = [x, gin, wdkv, glat, cos, sin]
    in_specs = [tok(D_MODEL), _const_spec(gin.shape), _const_spec(wdkv.shape),
                _const_spec(glat.shape), pos(QK_ROPE), pos(QK_ROPE)]
    out_shape = [jax.ShapeDtypeStruct((t, KV_LORA), F32), jax.ShapeDtypeStruct((t, QK_ROPE), F32)]
    out_specs = [tok(KV_LORA), tok(QK_ROPE)]
    if expand:
        args += [wuk, wuv]
        in_specs += [_const_spec(wuk.shape), _const_spec(wuv.shape)]
        nb = t // seq
        head_major = lambda w: pl.BlockSpec(
            (1, N_HEADS, tm, w), lambda i: (i // per_seq, 0, i % per_seq, 0))
        out_shape += [jax.ShapeDtypeStruct((nb, N_HEADS, seq, QK_DIM), BF16),
                      jax.ShapeDtypeStruct((nb, N_HEADS, seq, V_EXT), BF16)]
        out_specs += [head_major(QK_DIM), head_major(V_EXT)]
    return pl.pallas_call(
        functools.partial(_latent_kernel, expand=expand),
        grid=(t // tm,),
        in_specs=in_specs,
        out_specs=out_specs,
        out_shape=out_shape,
        compiler_params=_params(1),
        name="latent_expand" if expand else "latent",
    )(*args)


def _q_kernel(*refs, absorb):
    if absorb:
        x_ref, gpre_ref, wdq_ref, gq_ref, wuq_ref, cos_ref, sin_ref, wukt_ref, q_ref = refs
    else:
        x_ref, gpre_ref, wdq_ref, gq_ref, wuq_ref, cos_ref, sin_ref, q_ref = refs
    h = _rms(x_ref[...], gpre_ref[...]).astype(BF16)
    cq = _rms(_dot(h, wdq_ref[...]), gq_ref[...]).astype(BF16)
    q = _dot(cq, wuq_ref[...])
    qn = q[:, :N_HEADS * QK_NOPE]
    qr = q[:, N_HEADS * QK_NOPE:]
    width = N_HEADS * QK_ROPE
    half = QK_ROPE // 2
    lane = lax.broadcasted_iota(jnp.int32, qr.shape, 1)
    first_half = (lane & (QK_ROPE - 1)) < half
    swapped = jnp.where(first_half, pltpu.roll(qr, width - half, 1), pltpu.roll(qr, half, 1))
    qr = qr * cos_ref[...] + swapped * sin_ref[...]
    scale = SM_SCALE if absorb else SM_SCALE * LOG2E
    for hd in range(N_HEADS):
        qn_h = qn[:, hd * QK_NOPE:(hd + 1) * QK_NOPE]
        qr_h = (qr[:, hd * QK_ROPE:(hd + 1) * QK_ROPE] * scale).astype(BF16)
        if absorb:
            q_lat = _dot(qn_h.astype(BF16), wukt_ref[hd])
            q_ref[hd, :, 0:KV_LORA] = (q_lat * scale).astype(BF16)
            q_ref[hd, :, KV_LORA:LAT_DIM] = qr_h
        else:
            q_ref[0, hd, :, 0:QK_NOPE] = (qn_h * scale).astype(BF16)
            q_ref[0, hd, :, QK_NOPE:QK_DIM] = qr_h


def _q_call(x, gpre, wdq, gq, wuq, cos, sin, wukt, *, tm, seq, absorb):
    t = x.shape[0]
    per_seq = seq // tm
    width = N_HEADS * QK_ROPE
    tok = lambda w: pl.BlockSpec((tm, w), lambda i: (i, 0))
    pos = lambda w: pl.BlockSpec((tm, w), lambda i: (i % per_seq, 0))
    args = [x, gpre, wdq, gq, wuq, cos, sin]
    in_specs = [tok(D_MODEL), _const_spec(gpre.shape), _const_spec(wdq.shape),
                _const_spec(gq.shape), _const_spec(wuq.shape), pos(width), pos(width)]
    if absorb:
        args.append(wukt)
        in_specs.append(_const_spec(wukt.shape))
        out_shape = jax.ShapeDtypeStruct((N_HEADS, t, LAT_DIM), BF16)
        out_spec = pl.BlockSpec((N_HEADS, tm, LAT_DIM), lambda i: (0, i, 0))
    else:
        out_shape = jax.ShapeDtypeStruct((t // seq, N_HEADS, seq, QK_DIM), BF16)
        out_spec = pl.BlockSpec((1, N_HEADS, tm, QK_DIM),
                                lambda i: (i // per_seq, 0, i % per_seq, 0))
    return pl.pallas_call(
        functools.partial(_q_kernel, absorb=absorb),
        grid=(t // tm,),
        in_specs=in_specs,
        out_specs=out_spec,
        out_shape=out_shape,
        compiler_params=_params(1),
        name="q_absorb" if absorb else "q_heads",
    )(*args)


def _prompt_attn_kernel(q_ref, k_ref, v_ref, o_ref, m_ref, acc_ref, *, tq, tk, hb):
    i = pl.program_id(2)
    n_full = (i * tq) // tk
    m_ref[...] = jnp.full(m_ref.shape, -jnp.inf, F32)
    acc_ref[...] = jnp.zeros(acc_ref.shape, F32)

    def step(start, mask):
        for hh in range(hb):
            k = k_ref[0, hh, pl.ds(start, tk), :]
            v = v_ref[0, hh, pl.ds(start, tk), :]
            s = _dot_nt(q_ref[0, hh], k)
            if mask is not None:
                s = jnp.where(mask, s, -jnp.inf)
            m_prev = m_ref[hh]
            m_next = jnp.maximum(m_prev, jnp.max(s, axis=-1, keepdims=True))
            alpha = jnp.exp2(m_prev - m_next)
            p = jnp.exp2(s - jnp.concatenate([m_next] * (tk // LANES), axis=1))
            acc_ref[hh] = (acc_ref[hh] * jnp.concatenate([alpha] * (V_EXT // LANES), axis=1)
                           + _dot(p.astype(BF16), v))
            m_ref[hh] = m_next

    def full_step(j, carry):
        step(pl.multiple_of(j * tk, tk), None)
        return carry

    lax.fori_loop(0, n_full, full_step, 0)
    row = lax.broadcasted_iota(jnp.int32, (tq, tk), 0)
    col = lax.broadcasted_iota(jnp.int32, (tq, tk), 1)
    for d in range(tq // tk):
        step(pl.multiple_of(i * tq + d * tk, tk), col + d * tk <= row)
    for hh in range(hb):
        acc = acc_ref[hh]
        o = acc[:, :V_HEAD] / acc[:, V_HEAD:V_HEAD + 1]
        o_ref[0, :, hh * V_HEAD:(hh + 1) * V_HEAD] = o.astype(o_ref.dtype)


def _prompt_attn_call(q, k, v, *, tq, tk, hb):
    nb, nh, seq, _ = q.shape
    return pl.pallas_call(
        functools.partial(_prompt_attn_kernel, tq=tq, tk=tk, hb=hb),
        grid=(nb, nh // hb, seq // tq),
        in_specs=[pl.BlockSpec((1, hb, tq, QK_DIM), lambda b, h, i: (b, h, i, 0)),
                  pl.BlockSpec((1, hb, seq, QK_DIM), lambda b, h, i: (b, h, 0, 0)),
                  pl.BlockSpec((1, hb, seq, V_EXT), lambda b, h, i: (b, h, 0, 0))],
        out_specs=pl.BlockSpec((1, tq, hb * V_HEAD), lambda b, h, i: (b, i, h)),
        out_shape=jax.ShapeDtypeStruct((nb, seq, nh * V_HEAD), BF16),
        scratch_shapes=[pltpu.VMEM((hb, tq, LANES), F32),
                        pltpu.VMEM((hb, tq, V_EXT), F32)],
        compiler_params=_params(3),
        name="prompt_attn",
    )(q, k, v)


def _sample_attn_kernel(pt_ref, q_ref, cnew_ref, krnew_ref, cache_c_ref, cache_krt_ref, o_ref,
                        cbuf_ref, krbuf_ref, sem_ref, kc_ref, krt_ref, m_ref, l_ref, acc_ref,
                        *, n_pages, ch):
    b = pl.program_id(0)
    n_rows = pl.num_programs(0)
    n_chunks = n_pages // ch
    page = cbuf_ref.shape[1] // ch

    def chunk_copies(row, chunk, slot):
        copies = []
        for r in range(ch):
            pg = pt_ref[row * n_pages + chunk * ch + r]
            copies.append(pltpu.make_async_copy(
                cache_c_ref.at[pg], cbuf_ref.at[slot, pl.ds(r * page, page), :],
                sem_ref.at[slot, r]))
            copies.append(pltpu.make_async_copy(
                cache_krt_ref.at[pg], krbuf_ref.at[slot, r], sem_ref.at[slot, ch + r]))
        return copies

    @pl.when(b == 0)
    def _():
        for cp in chunk_copies(0, 0, 0):
            cp.start()

    q = q_ref[0]
    q_lat = q[:, :KV_LORA]
    q_rope = q[:, KV_LORA:]
    c_new = cnew_ref[0]
    s_new = (jnp.sum(q_lat.astype(F32) * c_new, axis=-1, keepdims=True)
             + jnp.sum(q_rope.astype(F32) * krnew_ref[0], axis=-1, keepdims=True))
    m_ref[...] = s_new
    l_ref[...] = jnp.ones_like(l_ref)
    acc_ref[...] = jnp.broadcast_to(c_new, acc_ref.shape)

    for chunk in range(n_chunks):
        slot = chunk % 2
        if chunk + 1 < n_chunks:
            for cp in chunk_copies(b, chunk + 1, 1 - slot):
                cp.start()
        else:
            @pl.when(b + 1 < n_rows)
            def _():
                for cp in chunk_copies(b + 1, 0, 1 - slot):
                    cp.start()
        for cp in chunk_copies(b, chunk, slot):
            cp.wait()
        kc_ref[...] = cbuf_ref[slot].astype(BF16)
        for r in range(ch):
            krt_ref[:, r * page:(r + 1) * page] = krbuf_ref[slot, r].astype(BF16)
        kc = kc_ref[...]
        s = _dot_nt(q_lat, kc) + _dot(q_rope, krt_ref[...])
        m = m_ref[...]
        m_new = jnp.maximum(m, jnp.max(s, axis=-1, keepdims=True))
        alpha = jnp.exp(m - m_new)
        p = jnp.exp(s - m_new)
        l_ref[...] = alpha * l_ref[...] + jnp.sum(p, axis=-1, keepdims=True)
        acc_ref[...] = alpha * acc_ref[...] + _dot(p.astype(BF16), kc)
        m_ref[...] = m_new

    o_ref[0] = (acc_ref[...] / l_ref[...]).astype(o_ref.dtype)


def _sample_attn_call(page_table, q, c_new, kr_new, cache_c, cache_krt, *, ch):
    nb, n_pages = page_table.shape
    page = cache_c.shape[1]
    n_chunks = n_pages // ch
    assert n_pages % ch == 0 and n_chunks % 2 == 0
    row = lambda h, w: pl.BlockSpec((1, h, w), lambda b, pt: (b, 0, 0))
    return pl.pallas_call(
        functools.partial(_sample_attn_kernel, n_pages=n_pages, ch=ch),
        grid_spec=pltpu.PrefetchScalarGridSpec(
            num_scalar_prefetch=1,
            grid=(nb,),
            in_specs=[row(N_HEADS, LAT_DIM), row(1, KV_LORA), row(1, QK_ROPE),
                      pl.BlockSpec(memory_space=pl.ANY), pl.BlockSpec(memory_space=pl.ANY)],
            out_specs=row(N_HEADS, KV_LORA),
            scratch_shapes=[pltpu.VMEM((2, ch * page, KV_LORA), F32),
                            pltpu.VMEM((2, ch, QK_ROPE, page), F32),
                            pltpu.SemaphoreType.DMA((2, 2 * ch)),
                            pltpu.VMEM((ch * page, KV_LORA), BF16),
                            pltpu.VMEM((QK_ROPE, ch * page), BF16),
                            pltpu.VMEM((N_HEADS, 1), F32),
                            pltpu.VMEM((N_HEADS, 1), F32),
                            pltpu.VMEM((N_HEADS, KV_LORA), F32)],
        ),
        out_shape=jax.ShapeDtypeStruct((nb, N_HEADS, KV_LORA), BF16),
        compiler_params=_params(1),
        name="sample_attn",
    )(page_table.reshape(-1), q, c_new.reshape(nb, 1, KV_LORA), kr_new.reshape(nb, 1, QK_ROPE),
      cache_c, cache_krt)


def _uv_kernel(o_ref, wuv_ref, out_ref):
    for hd in range(N_HEADS):
        out_ref[:, hd * V_HEAD:(hd + 1) * V_HEAD] = _dot(o_ref[hd], wuv_ref[hd]).astype(out_ref.dtype)


def _uv_call(o_lat, wuv_heads):
    nh, t, _ = o_lat.shape
    return pl.pallas_call(
        _uv_kernel,
        out_shape=jax.ShapeDtypeStruct((t, nh * V_HEAD), BF16),
        compiler_params=pltpu.CompilerParams(vmem_limit_bytes=VMEM_LIMIT_BYTES),
        name="uv_proj",
    )(o_lat, wuv_heads)


def _rope_tables(pos, reps):
    half = QK_ROPE // 2
    inv = ROPE_BASE ** (-jnp.arange(half, dtype=F32) / half)
    ang = pos.astype(F32)[:, None] * inv[None, :]
    cos = jnp.cos(ang)
    sin = jnp.sin(ang)
    cos64 = jnp.concatenate([cos, cos], axis=1)
    sin64 = jnp.concatenate([-sin, sin], axis=1)
    return cos64, sin64, jnp.tile(cos64, (1, reps)), jnp.tile(sin64, (1, reps))


def _row(v):
    return v.reshape(1, -1).astype(F32)


@jax.jit
def _forward(x_prompt, x_sample, cache_kv_latent, cache_k_rope, page_table,
             norm_pre_mix, norm_post_mix, norm_pre_ffn, norm_post_ffn,
             a_w_in, a_ln_g, a_ln_b, a_w_s, a_b_s, a_w_out,
             kv_norm_in, w_dkv, kv_latent_norm, w_uk, w_uv,
             b_w_dq, b_q_norm, b_w_uq, b_w_o, ffn_w_up, ffn_w_down):
    nb, seq, _ = x_prompt.shape
    db, dseq, _ = x_sample.shape
    n_pages = page_table.shape[1]
    n_past = n_pages * cache_kv_latent.shape[1]

    bf = lambda w: w.astype(BF16)
    w_in = bf(a_w_in)
    w_out = bf(a_w_out)
    w_up = bf(ffn_w_up)
    w_down = bf(ffn_w_down)
    w_dkv_b = bf(w_dkv)
    w_uk2d = bf(w_uk.reshape(KV_LORA, N_HEADS * QK_NOPE))
    w_uv2d = bf(w_uv.reshape(KV_LORA, N_HEADS * V_HEAD))
    w_ukt = bf(jnp.transpose(w_uk, (1, 2, 0)))
    w_uv_heads = bf(jnp.transpose(w_uv, (1, 0, 2)))
    w_dq = bf(b_w_dq)
    w_o = bf(b_w_o)
    cache_krt = jnp.transpose(cache_k_rope, (0, 2, 1))
    w_uq4 = b_w_uq.reshape(N_B_LAYERS, Q_LORA, N_HEADS, QK_DIM)
    w_uq = bf(jnp.concatenate(
        [w_uq4[..., :QK_NOPE].reshape(N_B_LAYERS, Q_LORA, N_HEADS * QK_NOPE),
         w_uq4[..., QK_NOPE:].reshape(N_B_LAYERS, Q_LORA, N_HEADS * QK_ROPE)], axis=-1))

    bias_chunk = jnp.repeat(jnp.transpose(a_b_s, (0, 2, 1)), GROUP_DIM, axis=2)
    bias_single = jnp.repeat(a_b_s[:, :, 0], GROUP_DIM, axis=1)[:, None, :]
    w_single = jnp.repeat(a_w_s[:, :, 0, 0], GROUP_DIM, axis=1)[:, None, :]

    def trunk(x, pos, *, tm, seq_len, single):
        cos64, sin64, cos_w, sin_w = _rope_tables(pos, N_HEADS)
        v_rows = []
        c = kr = kfull = vfull = None
        for l in range(N_A_LAYERS + N_B_LAYERS):
            if l < N_A_LAYERS:
                outs = _mixer_call(
                    x, _row(norm_pre_mix[l]), w_in[l], _row(a_ln_g[l]), _row(a_ln_b[l]),
                    w_single[l] if single else a_w_s[l],
                    bias_single[l] if single else bias_chunk[l],
                    w_out[l], _row(norm_post_mix[l]), tm=tm, single=single)
                x = outs[0]
                if single:
                    v_rows.append(outs[1])
                attn = None
            else:
                j = l - N_A_LAYERS
                if j == 0:
                    outs = _latent_call(
                        x, _row(kv_norm_in), w_dkv_b, _row(kv_latent_norm), cos64, sin64,
                        w_uk2d, w_uv2d, tm=tm, seq=seq_len, expand=not single)
                    c, kr = outs[0], outs[1]
                    if not single:
                        kfull, vfull = outs[2], outs[3]
                q = _q_call(x, _row(norm_pre_mix[l]), w_dq[j], _row(b_q_norm[j]), w_uq[j],
                            cos_w, sin_w, w_ukt, tm=tm, seq=seq_len, absorb=single)
                if single:
                    o_lat = _sample_attn_call(
                        page_table, jnp.transpose(q, (1, 0, 2)), c, kr,
                        cache_kv_latent, cache_krt, ch=PAGES_PER_CHUNK)
                    attn = _uv_call(jnp.transpose(o_lat, (1, 0, 2)), w_uv_heads)
                else:
                    attn = _prompt_attn_call(q, kfull, vfull, tq=ATTN_Q_TILE, tk=ATTN_KV_TILE,
                                             hb=ATTN_HEADS_PER_STEP)
                    attn = attn.reshape(x.shape[0], N_HEADS * V_HEAD)
            x = _tail_call(
                x, attn, None if attn is None else w_o[l - N_A_LAYERS],
                None if attn is None else _row(norm_post_mix[l]),
                _row(norm_pre_ffn[l]), w_up[l], w_down[l], _row(norm_post_ffn[l]), tm=tm)
        return x, c, kr, v_rows

    pos_p = jnp.arange(seq, dtype=jnp.int32)
    y_p, c_p, kr_p, _ = trunk(x_prompt.reshape(nb * seq, D_MODEL), pos_p,
                              tm=PROMPT_TOKEN_TILE, seq_len=seq, single=False)
    pos_s = jnp.full((db * dseq,), n_past, dtype=jnp.int32)
    y_s, c_s, kr_s, v_rows = trunk(x_sample.reshape(db * dseq, D_MODEL), pos_s,
                                   tm=db * dseq, seq_len=db * dseq, single=True)
    gate_v = jnp.stack(v_rows, axis=0).reshape(N_A_LAYERS, db, dseq, D_GATE)
    return (y_p.reshape(nb, seq, D_MODEL), y_s.reshape(db, dseq, D_MODEL),
            c_p.reshape(nb, seq, KV_LORA), kr_p.reshape(nb, seq, QK_ROPE),
            c_s.reshape(db, dseq, KV_LORA), kr_s.reshape(db, dseq, QK_ROPE), gate_v)


def kernel(x_prompt, x_sample, cache_kv_latent, cache_k_rope, page_table, norm_pre_mix, norm_post_mix, norm_pre_ffn, norm_post_ffn, a_w_in, a_ln_g, a_ln_b, a_w_s, a_b_s, a_w_out, kv_norm_in, w_dkv, kv_latent_norm, w_uk, w_uv, b_w_dq, b_q_norm, b_w_uq, b_w_o, ffn_w_up, ffn_w_down):
    assert x_sample.shape[1] == 1, "sample path assumes one new token per row"
    return _forward(x_prompt, x_sample, cache_kv_latent, cache_k_rope, page_table,
                    norm_pre_mix, norm_post_mix, norm_pre_ffn, norm_post_ffn,
                    a_w_in, a_ln_g, a_ln_b, a_w_s, a_b_s, a_w_out,
                    kv_norm_in, w_dkv, kv_latent_norm, w_uk, w_uv,
                    b_w_dq, b_q_norm, b_w_uq, b_w_o, ffn_w_up, ffn_w_down)
```

```python
import functools

import jax
import jax.numpy as jnp
from jax import lax
from jax.experimental import pallas as pl
from jax.experimental.pallas import tpu as pltpu

D_MODEL = 1024
CHUNK = 128
D_GATE = 2 * D_MODEL
N_GROUPS = 8
GROUP_DIM = D_GATE // N_GROUPS
D_FF = 4 * D_MODEL
N_HEADS = 16
QK_NOPE = 128
QK_ROPE = 64
QK_DIM = QK_NOPE + QK_ROPE
V_HEAD = 128
Q_LORA = 256
KV_LORA = 512
LAT_DIM = KV_LORA + QK_ROPE
ROPE_BASE = 10000.0
EPS = 1e-6
SM_SCALE = QK_DIM ** -0.5
N_A_LAYERS = 2
N_B_LAYERS = 2

F32 = jnp.float32
BF16 = jnp.bfloat16

VMEM_LIMIT_BYTES = 56 * 1024 * 1024

LANES = 128
V_EXT = 2 * LANES
LOG2E = 1.4426950408889634

PROMPT_TOKEN_TILE = 512
ATTN_Q_TILE = 512
ATTN_KV_TILE = 512
ATTN_HEADS_PER_STEP = 8
PAGES_PER_CHUNK = 16
PAGE_RING_DEPTH = 4


def _dot(a, b):
    return jnp.dot(a, b, preferred_element_type=F32)


def _dot_nt(a, b):
    return lax.dot_general(a, b, (((1,), (1,)), ((), ())), preferred_element_type=F32)


def _rms(x, g):
    return x * lax.rsqrt(jnp.mean(x * x, axis=-1, keepdims=True) + EPS) * g


def _gelu(z):
    return 0.5 * z * (1.0 + lax.erf(z * (0.5 ** 0.5)))


def _const_spec(shape):
    n = len(shape)
    return pl.BlockSpec(shape, lambda *_: (0,) * n, pipeline_mode=pl.Buffered(1))


def _resident(w):
    if isinstance(w, tuple):
        arr, layer = w
        n = arr.ndim - 1
        return arr, pl.BlockSpec((None,) + arr.shape[1:], lambda *_: (layer,) + (0,) * n,
                                 pipeline_mode=pl.Buffered(1))
    return w, _const_spec(w.shape)


def _params(n_grid):
    return pltpu.CompilerParams(
        dimension_semantics=("arbitrary",) * n_grid,
        vmem_limit_bytes=VMEM_LIMIT_BYTES,
    )


def _mixer_kernel(x_ref, gpre_ref, win_ref, lng_ref, lnb_ref, ws_ref, bias_ref, wout_ref,
                  gpost_ref, *rest, tm, single):
    if single:
        o_ref, v_ref, gated_ref = rest
    else:
        o_ref, gated_ref = rest
    x = x_ref[...]
    h = _rms(x, gpre_ref[...]).astype(BF16)
    z = _gelu(_dot(h, win_ref[...]))
    u = z[:, :D_GATE]
    v = z[:, D_GATE:]
    mu = jnp.mean(v, axis=-1, keepdims=True)
    vc = v - mu
    v = vc * lax.rsqrt(jnp.mean(vc * vc, axis=-1, keepdims=True) + EPS) * lng_ref[...] + lnb_ref[...]
    if single:
        v_ref[...] = v
        gated_ref[...] = (u * (v * ws_ref[...] + bias_ref[...])).astype(BF16)
    else:
        vb = v.astype(BF16)
        row = lax.broadcasted_iota(jnp.int32, (CHUNK, CHUNK), 0)
        col = lax.broadcasted_iota(jnp.int32, (CHUNK, CHUNK), 1)
        causal = row >= col
        for g in range(N_GROUPS):
            wg = jnp.where(causal, ws_ref[g], 0.0).astype(BF16)
            cols = slice(g * GROUP_DIM, (g + 1) * GROUP_DIM)
            for c in range(tm // CHUNK):
                rows = slice(c * CHUNK, (c + 1) * CHUNK)
                s = _dot(wg, vb[rows, cols]) + bias_ref[:, cols]
                gated_ref[rows, cols] = (u[rows, cols] * s).astype(BF16)
    m = _dot(gated_ref[...], wout_ref[...])
    o_ref[...] = x + _rms(m, gpost_ref[...])


def _mixer_call(x, gpre, win, lng, lnb, ws, bias, wout, gpost, *, tm, single):
    t = x.shape[0]
    tok = lambda w: pl.BlockSpec((tm, w), lambda i: (i, 0))
    out_shape = [jax.ShapeDtypeStruct((t, D_MODEL), F32)]
    out_specs = [tok(D_MODEL)]
    if single:
        out_shape.append(jax.ShapeDtypeStruct((t, D_GATE), F32))
        out_specs.append(tok(D_GATE))
    weights, weight_specs = zip(*map(_resident, (gpre, win, lng, lnb, ws, bias, wout, gpost)))
    return pl.pallas_call(
        functools.partial(_mixer_kernel, tm=tm, single=single),
        grid=(t // tm,),
        in_specs=[tok(D_MODEL), *weight_specs],
        out_specs=out_specs,
        out_shape=out_shape,
        scratch_shapes=[pltpu.VMEM((tm, D_GATE), BF16)],
        compiler_params=_params(1),
        name="mixer_single" if single else "mixer_chunk",
    )(x, *weights)


def _tail_kernel(*refs, has_oproj):
    if has_oproj:
        x_ref, a_ref, wo_ref, gpm_ref, gpf_ref, wup_ref, wdn_ref, gpo_ref, o_ref = refs
    else:
        x_ref, gpf_ref, wup_ref, wdn_ref, gpo_ref, o_ref = refs
    x = x_ref[...]
    if has_oproj:
        x = x + _rms(_dot(a_ref[...], wo_ref[...]), gpm_ref[...])
    h = _rms(x, gpf_ref[...]).astype(BF16)
    a = jnp.square(jnp.maximum(_dot(h, wup_ref[...]), 0.0)).astype(BF16)
    o_ref[...] = x + _rms(_dot(a, wdn_ref[...]), gpo_ref[...])


def _tail_call(x, attn, wo, gpm, gpf, wup, wdn, gpo, *, tm):
    t = x.shape[0]
    tok = lambda w: pl.BlockSpec((tm, w), lambda i: (i, 0))
    has_oproj = attn is not None
    if has_oproj:
        weights, weight_specs = zip(*map(_resident, (wo, gpm, gpf, wup, wdn, gpo)))
        args = (x, attn, *weights)
        in_specs = [tok(D_MODEL), tok(attn.shape[1]), *weight_specs]
    else:
        weights, weight_specs = zip(*map(_resident, (gpf, wup, wdn, gpo)))
        args = (x, *weights)
        in_specs = [tok(D_MODEL), *weight_specs]
    return pl.pallas_call(
        functools.partial(_tail_kernel, has_oproj=has_oproj),
        grid=(t // tm,),
        in_specs=in_specs,
        out_specs=tok(D_MODEL),
        out_shape=jax.ShapeDtypeStruct((t, D_MODEL), F32),
        compiler_params=_params(1),
        name="tail_oproj" if has_oproj else "tail_ffn",
    )(*args)


def _latent_kernel(*refs, expand):
    if expand:
        (x_ref, gin_ref, wdkv_ref, glat_ref, cos_ref, sin_ref, wuk_ref, wuv_ref,
         c_ref, kr_ref, kfull_ref, v_ref) = refs
    else:
        x_ref, gin_ref, wdkv_ref, glat_ref, cos_ref, sin_ref, c_ref, kr_ref = refs
    h = _rms(x_ref[...], gin_ref[...]).astype(BF16)
    ckr = _dot(h, wdkv_ref[...])
    c = _rms(ckr[:, :KV_LORA], glat_ref[...])
    kr = ckr[:, KV_LORA:LAT_DIM] * cos_ref[...] + ckr[:, LAT_DIM:] * sin_ref[...]
    c_ref[...] = c
    kr_ref[...] = kr
    if expand:
        cb = c.astype(BF16)
        krb = kr.astype(BF16)
        kn = _dot(cb, wuk_ref[...])
        vv = _dot(cb, wuv_ref[...])
        lane = lax.broadcasted_iota(jnp.int32, (cb.shape[0], V_EXT - V_HEAD), 1)
        ones_col = jnp.where(lane == 0, 1.0, 0.0).astype(BF16)
        for hd in range(N_HEADS):
            kfull_ref[0, hd, :, 0:QK_NOPE] = kn[:, hd * QK_NOPE:(hd + 1) * QK_NOPE].astype(BF16)
            kfull_ref[0, hd, :, QK_NOPE:QK_DIM] = krb
            v_ref[0, hd, :, 0:V_HEAD] = vv[:, hd * V_HEAD:(hd + 1) * V_HEAD].astype(BF16)
            v_ref[0, hd, :, V_HEAD:V_EXT] = ones_col


def _latent_call(x, gin, wdkv, glat, cos, sin, wuk, wuv, *, tm, seq, expand):
    t = x.shape[0]
    per_seq = seq // tm
    tok = lambda w: pl.BlockSpec((tm, w), lambda i: (i, 0))
    pos = lambda w: pl.BlockSpec((tm, w), lambda i: (i % per_seq, 0))
    args = [x, gin, wdkv, glat, cos, sin]
    in_specs = [tok(D_MODEL), _const_spec(gin.shape), _const_spec(wdkv.shape),
                _const_spec(glat.shape), pos(QK_ROPE), pos(QK_ROPE)]
    out_shape = [jax.ShapeDtypeStruct((t, KV_LORA), F32), jax.ShapeDtypeStruct((t, QK_ROPE), F32)]
    out_specs = [tok(KV_LORA), tok(QK_ROPE)]
    if expand:
        args += [wuk, wuv]
        in_specs += [_const_spec(wuk.shape), _const_spec(wuv.shape)]
        nb = t // seq
        head_major = lambda w: pl.BlockSpec(
            (1, N_HEADS, tm, w), lambda i: (i // per_seq, 0, i % per_seq, 0))
        out_shape += [jax.ShapeDtypeStruct((nb, N_HEADS, seq, QK_DIM), BF16),
                      jax.ShapeDtypeStruct((nb, N_HEADS, seq, V_EXT), BF16)]
        out_specs += [head_major(QK_DIM), head_major(V_EXT)]
    return pl.pallas_call(
        functools.partial(_latent_kernel, expand=expand),
        grid=(t // tm,),
        in_specs=in_specs,
        out_specs=out_specs,
        out_shape=out_shape,
        compiler_params=_params(1),
        name="latent_expand" if expand else "latent",
    )(*args)


def _q_kernel(*refs, absorb):
    if absorb:
        x_ref, gpre_ref, wdq_ref, gq_ref, wuq_ref, cos_ref, sin_ref, wukt_ref, q_ref = refs
    else:
        x_ref, gpre_ref, wdq_ref, gq_ref, wuq_ref, cos_ref, sin_ref, q_ref = refs
    h = _rms(x_ref[...], gpre_ref[...]).astype(BF16)
    cq = _rms(_dot(h, wdq_ref[...]), gq_ref[...]).astype(BF16)
    q = _dot(cq, wuq_ref[...])
    n_nope = N_HEADS * QK_NOPE
    n_rope = N_HEADS * QK_ROPE
    qn = q[:, :n_nope]
    qr = q[:, n_nope:n_nope + n_rope] * cos_ref[...] + q[:, n_nope + n_rope:] * sin_ref[...]
    scale = SM_SCALE if absorb else SM_SCALE * LOG2E
    for hd in range(N_HEADS):
        qn_h = qn[:, hd * QK_NOPE:(hd + 1) * QK_NOPE]
        qr_h = (qr[:, hd * QK_ROPE:(hd + 1) * QK_ROPE] * scale).astype(BF16)
        if absorb:
            q_lat = _dot(qn_h.astype(BF16), wukt_ref[hd])
            q_ref[hd, :, 0:KV_LORA] = (q_lat * scale).astype(BF16)
            q_ref[hd, :, KV_LORA:LAT_DIM] = qr_h
        else:
            q_ref[0, hd, :, 0:QK_NOPE] = (qn_h * scale).astype(BF16)
            q_ref[0, hd, :, QK_NOPE:QK_DIM] = qr_h


def _q_call(x, gpre, wdq, gq, wuq, cos, sin, wukt, *, tm, seq, absorb):
    t = x.shape[0]
    per_seq = seq // tm
    width = N_HEADS * QK_ROPE
    tok = lambda w: pl.BlockSpec((tm, w), lambda i: (i, 0))
    pos = lambda w: pl.BlockSpec((tm, w), lambda i: (i % per_seq, 0))
    weights, weight_specs = zip(*map(_resident, (gpre, wdq, gq, wuq)))
    args = [x, *weights, cos, sin]
    in_specs = [tok(D_MODEL), *weight_specs, pos(width), pos(width)]
    if absorb:
        args.append(wukt)
        in_specs.append(_const_spec(wukt.shape))
        out_shape = jax.ShapeDtypeStruct((N_HEADS, t, LAT_DIM), BF16)
        out_spec = pl.BlockSpec((N_HEADS, tm, LAT_DIM), lambda i: (0, i, 0))
    else:
        out_shape = jax.ShapeDtypeStruct((t // seq, N_HEADS, seq, QK_DIM), BF16)
        out_spec = pl.BlockSpec((1, N_HEADS, tm, QK_DIM),
                                lambda i: (i // per_seq, 0, i % per_seq, 0))
    return pl.pallas_call(
        functools.partial(_q_kernel, absorb=absorb),
        grid=(t // tm,),
        in_specs=in_specs,
        out_specs=out_spec,
        out_shape=out_shape,
        compiler_params=_params(1),
        name="q_absorb" if absorb else "q_heads",
    )(*args)


def _prompt_attn_kernel(q_ref, k_ref, v_ref, o_ref, m_ref, acc_ref, *, tq, tk, hb):
    i = pl.program_id(2)
    n_full = (i * tq) // tk
    m_ref[...] = jnp.full(m_ref.shape, -jnp.inf, F32)
    acc_ref[...] = jnp.zeros(acc_ref.shape, F32)

    def step(start, mask):
        for hh in range(hb):
            k = k_ref[0, hh, pl.ds(start, tk), :]
            v = v_ref[0, hh, pl.ds(start, tk), :]
            s = _dot_nt(q_ref[0, hh], k)
            if mask is not None:
                s = jnp.where(mask, s, -jnp.inf)
            m_prev = m_ref[hh]
            m_next = jnp.maximum(m_prev, jnp.max(s, axis=-1, keepdims=True))
            alpha = jnp.exp2(m_prev - m_next)
            p = jnp.exp2(s - jnp.concatenate([m_next] * (tk // LANES), axis=1))
            acc_ref[hh] = (acc_ref[hh] * jnp.concatenate([alpha] * (V_EXT // LANES), axis=1)
                           + _dot(p.astype(BF16), v))
            m_ref[hh] = m_next

    def full_step(j, carry):
        step(pl.multiple_of(j * tk, tk), None)
        return carry

    lax.fori_loop(0, n_full, full_step, 0)
    row = lax.broadcasted_iota(jnp.int32, (tq, tk), 0)
    col = lax.broadcasted_iota(jnp.int32, (tq, tk), 1)
    for d in range(tq // tk):
        step(pl.multiple_of(i * tq + d * tk, tk), col + d * tk <= row)
    for hh in range(hb):
        acc = acc_ref[hh]
        o = acc[:, :V_HEAD] / acc[:, V_HEAD:V_HEAD + 1]
        o_ref[0, :, hh * V_HEAD:(hh + 1) * V_HEAD] = o.astype(o_ref.dtype)


def _prompt_attn_call(q, k, v, *, tq, tk, hb):
    nb, nh, seq, _ = q.shape
    return pl.pallas_call(
        functools.partial(_prompt_attn_kernel, tq=tq, tk=tk, hb=hb),
        grid=(nb, nh // hb, seq // tq),
        in_specs=[pl.BlockSpec((1, hb, tq, QK_DIM), lambda b, h, i: (b, h, i, 0)),
                  pl.BlockSpec((1, hb, seq, QK_DIM), lambda b, h, i: (b, h, 0, 0)),
                  pl.BlockSpec((1, hb, seq, V_EXT), lambda b, h, i: (b, h, 0, 0))],
        out_specs=pl.BlockSpec((1, tq, hb * V_HEAD), lambda b, h, i: (b, i, h)),
        out_shape=jax.ShapeDtypeStruct((nb, seq, nh * V_HEAD), BF16),
        scratch_shapes=[pltpu.VMEM((hb, tq, LANES), F32),
                        pltpu.VMEM((hb, tq, V_EXT), F32)],
        compiler_params=_params(3),
        name="prompt_attn",
    )(q, k, v)


def _sample_attn_kernel(pt_ref, q_ref, cnew_ref, krnew_ref, cache_c_ref, cache_krt_ref, o_ref,
                        cbuf_ref, krbuf_ref, sem_ref, kc_ref, krt_ref,
                        *, n_pages, ch, depth):
    b = pl.program_id(0)
    n_rows = pl.num_programs(0)
    n_chunks = n_pages // ch
    page = cbuf_ref.shape[1] // ch

    def chunk_copies(row, chunk):
        slot = chunk % depth
        copies = []
        for r in range(ch):
            pg = pt_ref[row * n_pages + chunk * ch + r]
            copies.append(pltpu.make_async_copy(
                cache_c_ref.at[pg], cbuf_ref.at[slot, pl.ds(r * page, page), :],
                sem_ref.at[slot, r]))
            copies.append(pltpu.make_async_copy(
                cache_krt_ref.at[pg], krbuf_ref.at[slot, r], sem_ref.at[slot, ch + r]))
        return copies

    @pl.when(b == 0)
    def _():
        for chunk in range(depth - 1):
            for cp in chunk_copies(0, chunk):
                cp.start()

    q = q_ref[0]
    q_lat = q[:, :KV_LORA]
    q_rope = q[:, KV_LORA:]

    def scores(chunk):
        ahead = chunk + depth - 1
        if ahead < n_chunks:
            for cp in chunk_copies(b, ahead):
                cp.start()
        else:
            @pl.when(b + 1 < n_rows)
            def _():
                for cp in chunk_copies(b + 1, ahead - n_chunks):
                    cp.start()
        for cp in chunk_copies(b, chunk):
            cp.wait()
        slot = chunk % depth
        half = chunk % 2
        kc_ref[half] = cbuf_ref[slot].astype(BF16)
        for r in range(ch):
            krt_ref[half, :, r * page:(r + 1) * page] = krbuf_ref[slot, r].astype(BF16)
        return _dot_nt(q_lat, kc_ref[half]) + _dot(q_rope, krt_ref[half])

    c_new = cnew_ref[0]
    m = (jnp.sum(q_lat.astype(F32) * c_new, axis=-1, keepdims=True)
         + jnp.sum(q_rope.astype(F32) * krnew_ref[0], axis=-1, keepdims=True))
    l = jnp.ones_like(m)
    acc = jnp.broadcast_to(c_new, (N_HEADS, KV_LORA))

    s_next = scores(0)
    for chunk in range(n_chunks):
        s = s_next
        if chunk + 1 < n_chunks:
            s_next = scores(chunk + 1)
        m_new = jnp.maximum(m, jnp.max(s, axis=-1, keepdims=True))
        alpha = jnp.exp(m - m_new)
        p = jnp.exp(s - m_new)
        l = alpha * l + jnp.sum(p, axis=-1, keepdims=True)
        acc = alpha * acc + _dot(p.astype(BF16), kc_ref[chunk % 2])
        m = m_new

    o_ref[0] = (acc / l).astype(o_ref.dtype)


def _sample_attn_call(page_table, q, c_new, kr_new, cache_c, cache_krt, *, ch, depth):
    nb, n_pages = page_table.shape
    page = cache_c.shape[1]
    n_chunks = n_pages // ch
    assert n_pages % ch == 0 and n_chunks % depth == 0 and depth >= 2
    row = lambda h, w: pl.BlockSpec((1, h, w), lambda b, pt: (b, 0, 0))
    return pl.pallas_call(
        functools.partial(_sample_attn_kernel, n_pages=n_pages, ch=ch, depth=depth),
        grid_spec=pltpu.PrefetchScalarGridSpec(
            num_scalar_prefetch=1,
            grid=(nb,),
            in_specs=[row(N_HEADS, LAT_DIM), row(1, KV_LORA), row(1, QK_ROPE),
                      pl.BlockSpec(memory_space=pl.ANY), pl.BlockSpec(memory_space=pl.ANY)],
            out_specs=row(N_HEADS, KV_LORA),
            scratch_shapes=[pltpu.VMEM((depth, ch * page, KV_LORA), F32),
                            pltpu.VMEM((depth, ch, QK_ROPE, page), F32),
                            pltpu.SemaphoreType.DMA((depth, 2 * ch)),
                            pltpu.VMEM((2, ch * page, KV_LORA), BF16),
                            pltpu.VMEM((2, QK_ROPE, ch * page), BF16)],
        ),
        out_shape=jax.ShapeDtypeStruct((nb, N_HEADS, KV_LORA), BF16),
        compiler_params=_params(1),
        name="sample_attn",
    )(page_table.reshape(-1), q, c_new.reshape(nb, 1, KV_LORA), kr_new.reshape(nb, 1, QK_ROPE),
      cache_c, cache_krt)


def _uv_kernel(o_ref, wuv_ref, out_ref):
    for hd in range(N_HEADS):
        out_ref[:, hd * V_HEAD:(hd + 1) * V_HEAD] = _dot(o_ref[hd], wuv_ref[hd]).astype(out_ref.dtype)


def _uv_call(o_lat, wuv_heads):
    nh, t, _ = o_lat.shape
    return pl.pallas_call(
        _uv_kernel,
        out_shape=jax.ShapeDtypeStruct((t, nh * V_HEAD), BF16),
        compiler_params=pltpu.CompilerParams(vmem_limit_bytes=VMEM_LIMIT_BYTES),
        name="uv_proj",
    )(o_lat, wuv_heads)


def _rope_tables(pos, reps):
    half = QK_ROPE // 2
    inv = ROPE_BASE ** (-jnp.arange(half, dtype=F32) / half)
    ang = pos.astype(F32)[:, None] * inv[None, :]
    cos = jnp.cos(ang)
    sin = jnp.sin(ang)
    cos64 = jnp.concatenate([cos, cos], axis=1)
    sin64 = jnp.concatenate([-sin, sin], axis=1)
    return cos64, sin64, jnp.tile(cos64, (1, reps)), jnp.tile(sin64, (1, reps))


def _row(v):
    return v.reshape(1, -1).astype(F32)


@jax.jit
def _forward(x_prompt, x_sample, cache_kv_latent, cache_k_rope, page_table,
             norm_pre_mix, norm_post_mix, norm_pre_ffn, norm_post_ffn,
             a_w_in, a_ln_g, a_ln_b, a_w_s, a_b_s, a_w_out,
             kv_norm_in, w_dkv, kv_latent_norm, w_uk, w_uv,
             b_w_dq, b_q_norm, b_w_uq, b_w_o, ffn_w_up, ffn_w_down):
    nb, seq, _ = x_prompt.shape
    db, dseq, _ = x_sample.shape
    n_pages = page_table.shape[1]
    n_past = n_pages * cache_kv_latent.shape[1]

    bf = lambda w: w.astype(BF16)
    w_in = bf(a_w_in)
    w_out = bf(a_w_out)
    w_up = bf(ffn_w_up)
    w_down = bf(ffn_w_down)
    half = QK_ROPE // 2
    swap_halves = lambda w: jnp.concatenate([w[..., half:], w[..., :half]], axis=-1)
    w_dkv_b = bf(jnp.concatenate([w_dkv, swap_halves(w_dkv[:, KV_LORA:])], axis=1))
    w_uk2d = bf(w_uk.reshape(KV_LORA, N_HEADS * QK_NOPE))
    w_uv2d = bf(w_uv.reshape(KV_LORA, N_HEADS * V_HEAD))
    w_ukt = bf(jnp.transpose(w_uk, (1, 2, 0)))
    w_uv_heads = bf(jnp.transpose(w_uv, (1, 0, 2)))
    w_dq = bf(b_w_dq)
    w_o = bf(b_w_o)
    cache_krt = jnp.transpose(cache_k_rope, (0, 2, 1))
    w_uq4 = b_w_uq.reshape(N_B_LAYERS, Q_LORA, N_HEADS, QK_DIM)
    w_uq_rope = w_uq4[..., QK_NOPE:]
    w_uq = bf(jnp.concatenate(
        [w_uq4[..., :QK_NOPE].reshape(N_B_LAYERS, Q_LORA, N_HEADS * QK_NOPE),
         w_uq_rope.reshape(N_B_LAYERS, Q_LORA, N_HEADS * QK_ROPE),
         swap_halves(w_uq_rope).reshape(N_B_LAYERS, Q_LORA, N_HEADS * QK_ROPE)], axis=-1))

    bias_chunk = jnp.repeat(jnp.transpose(a_b_s, (0, 2, 1)), GROUP_DIM, axis=2)
    bias_single = jnp.repeat(a_b_s[:, :, 0], GROUP_DIM, axis=1)[:, None, :]
    w_single = jnp.repeat(a_w_s[:, :, 0, 0], GROUP_DIM, axis=1)[:, None, :]

    def trunk(x, pos, *, tm, seq_len, single):
        cos64, sin64, cos_w, sin_w = _rope_tables(pos, N_HEADS)
        v_rows = []
        c = kr = kfull = vfull = None
        for l in range(N_A_LAYERS + N_B_LAYERS):
            if l < N_A_LAYERS:
                outs = _mixer_call(
                    x, _row(norm_pre_mix[l]), (w_in, l), _row(a_ln_g[l]), _row(a_ln_b[l]),
                    (w_single if single else a_w_s, l),
                    (bias_single if single else bias_chunk, l),
                    (w_out, l), _row(norm_post_mix[l]), tm=tm, single=single)
                x = outs[0]
                if single:
                    v_rows.append(outs[1])
                attn = None
            else:
                j = l - N_A_LAYERS
                if j == 0:
                    outs = _latent_call(
                        x, _row(kv_norm_in), w_dkv_b, _row(kv_latent_norm), cos64, sin64,
                        w_uk2d, w_uv2d, tm=tm, seq=seq_len, expand=not single)
                    c, kr = outs[0], outs[1]
                    if not single:
                        kfull, vfull = outs[2], outs[3]
                q = _q_call(x, _row(norm_pre_mix[l]), (w_dq, j), _row(b_q_norm[j]), (w_uq, j),
                            cos_w, sin_w, w_ukt, tm=tm, seq=seq_len, absorb=single)
                if single:
                    o_lat = _sample_attn_call(
                        page_table, jnp.transpose(q, (1, 0, 2)), c, kr,
                        cache_kv_latent, cache_krt, ch=PAGES_PER_CHUNK,
                        depth=PAGE_RING_DEPTH)
                    attn = _uv_call(jnp.transpose(o_lat, (1, 0, 2)), w_uv_heads)
                else:
                    attn = _prompt_attn_call(q, kfull, vfull, tq=ATTN_Q_TILE, tk=ATTN_KV_TILE,
                                             hb=ATTN_HEADS_PER_STEP)
                    attn = attn.reshape(x.shape[0], N_HEADS * V_HEAD)
            x = _tail_call(
                x, attn, None if attn is None else (w_o, l - N_A_LAYERS),
                None if attn is None else _row(norm_post_mix[l]),
                _row(norm_pre_ffn[l]), (w_up, l), (w_down, l), _row(norm_post_ffn[l]), tm=tm)
        return x, c, kr, v_rows

    pos_p = jnp.arange(seq, dtype=jnp.int32)
    y_p, c_p, kr_p, _ = trunk(x_prompt.reshape(nb * seq, D_MODEL), pos_p,
                              tm=PROMPT_TOKEN_TILE, seq_len=seq, single=False)
    pos_s = jnp.full((db * dseq,), n_past, dtype=jnp.int32)
    y_s, c_s, kr_s, v_rows = trunk(x_sample.reshape(db * dseq, D_MODEL), pos_s,
                                   tm=db * dseq, seq_len=db * dseq, single=True)
    gate_v = jnp.stack(v_rows, axis=0).reshape(N_A_LAYERS, db, dseq, D_GATE)
    return (y_p.reshape(nb, seq, D_MODEL), y_s.reshape(db, dseq, D_MODEL),
            c_p.reshape(nb, seq, KV_LORA), kr_p.reshape(nb, seq, QK_ROPE),
            c_s.reshape(db, dseq, KV_LORA), kr_s.reshape(db, dseq, QK_ROPE), gate_v)


def kernel(x_prompt, x_sample, cache_kv_latent, cache_k_rope, page_table, norm_pre_mix, norm_post_mix, norm_pre_ffn, norm_post_ffn, a_w_in, a_ln_g, a_ln_b, a_w_s, a_b_s, a_w_out, kv_norm_in, w_dkv, kv_latent_norm, w_uk, w_uv, b_w_dq, b_q_norm, b_w_uq, b_w_o, ffn_w_up, ffn_w_down):
    assert x_sample.shape[1] == 1, "sample path assumes one new token per row"
    return _forward(x_prompt, x_sample, cache_kv_latent, cache_k_rope, page_table,
                    norm_pre_mix, norm_post_mix, norm_pre_ffn, norm_post_ffn,
                    a_w_in, a_ln_g, a_ln_b, a_w_s, a_b_s, a_w_out,
                    kv_norm_in, w_dkv, kv_latent_norm, w_uk, w_uv,
                    b_w_dq, b_q_norm, b_w_uq, b_w_o, ffn_w_up, ffn_w_down)
```

```python
import functools

import jax
import jax.numpy as jnp
from jax import lax
from jax.experimental import pallas as pl
from jax.experimental.pallas import tpu as pltpu

D_MODEL = 1024
CHUNK = 128
D_GATE = 2 * D_MODEL
N_GROUPS = 8
GROUP_DIM = D_GATE // N_GROUPS
D_FF = 4 * D_MODEL
N_HEADS = 16
QK_NOPE = 128
QK_ROPE = 64
QK_DIM = QK_NOPE + QK_ROPE
V_HEAD = 128
Q_LORA = 256
KV_LORA = 512
LAT_DIM = KV_LORA + QK_ROPE
ROPE_BASE = 10000.0
EPS = 1e-6
SM_SCALE = QK_DIM ** -0.5
N_A_LAYERS = 2
N_B_LAYERS = 2

F32 = jnp.float32
BF16 = jnp.bfloat16

VMEM_LIMIT_BYTES = 56 * 1024 * 1024

LANES = 128
V_EXT = 2 * LANES
LOG2E = 1.4426950408889634

PROMPT_TOKEN_TILE = 512
ATTN_Q_TILE = 512
ATTN_KV_TILE = 512
ATTN_HEADS_PER_STEP = 8
PAGES_PER_CHUNK = 16
PAGE_RING_DEPTH = 4


def _dot(a, b):
    return jnp.dot(a, b, preferred_element_type=F32)


def _dot_nt(a, b):
    return lax.dot_general(a, b, (((1,), (1,)), ((), ())), preferred_element_type=F32)


def _rms(x, g):
    return x * lax.rsqrt(jnp.mean(x * x, axis=-1, keepdims=True) + EPS) * g


def _gelu(z):
    return 0.5 * z * (1.0 + lax.erf(z * (0.5 ** 0.5)))


def _const_spec(shape):
    n = len(shape)
    return pl.BlockSpec(shape, lambda *_: (0,) * n, pipeline_mode=pl.Buffered(1))


def _resident(w):
    if isinstance(w, tuple):
        arr, layer = w
        n = arr.ndim - 1
        return arr, pl.BlockSpec((None,) + arr.shape[1:], lambda *_: (layer,) + (0,) * n,
                                 pipeline_mode=pl.Buffered(1))
    return w, _const_spec(w.shape)


def _params(n_grid):
    return pltpu.CompilerParams(
        dimension_semantics=("arbitrary",) * n_grid,
        vmem_limit_bytes=VMEM_LIMIT_BYTES,
    )


def _mixer_kernel(x_ref, gpre_ref, win_ref, lng_ref, lnb_ref, ws_ref, bias_ref, wout_ref,
                  gpost_ref, *rest, tm, single):
    if single:
        o_ref, v_ref, gated_ref = rest
    else:
        o_ref, gated_ref = rest
    x = x_ref[...]
    h = _rms(x, gpre_ref[...]).astype(BF16)
    v = _gelu(_dot(h, win_ref[:, D_GATE:]))
    u = _gelu(_dot(h, win_ref[:, :D_GATE]))
    mu = jnp.mean(v, axis=-1, keepdims=True)
    vc = v - mu
    v = vc * lax.rsqrt(jnp.mean(vc * vc, axis=-1, keepdims=True) + EPS) * lng_ref[...] + lnb_ref[...]
    if single:
        v_ref[...] = v
        gated_ref[...] = (u * (v * ws_ref[...] + bias_ref[...])).astype(BF16)
    else:
        vb = v.astype(BF16)
        row = lax.broadcasted_iota(jnp.int32, (CHUNK, CHUNK), 0)
        col = lax.broadcasted_iota(jnp.int32, (CHUNK, CHUNK), 1)
        causal = row >= col
        for g in range(N_GROUPS):
            wg = jnp.where(causal, ws_ref[g], 0.0).astype(BF16)
            cols = slice(g * GROUP_DIM, (g + 1) * GROUP_DIM)
            for c in range(tm // CHUNK):
                rows = slice(c * CHUNK, (c + 1) * CHUNK)
                s = _dot(wg, vb[rows, cols]) + bias_ref[:, cols]
                gated_ref[rows, cols] = (u[rows, cols] * s).astype(BF16)
    m = _dot(gated_ref[...], wout_ref[...])
    o_ref[...] = x + _rms(m, gpost_ref[...])


def _mixer_call(x, gpre, win, lng, lnb, ws, bias, wout, gpost, *, tm, single):
    t = x.shape[0]
    tok = lambda w: pl.BlockSpec((tm, w), lambda i: (i, 0))
    out_shape = [jax.ShapeDtypeStruct((t, D_MODEL), F32)]
    out_specs = [tok(D_MODEL)]
    if single:
        out_shape.append(jax.ShapeDtypeStruct((t, D_GATE), F32))
        out_specs.append(tok(D_GATE))
    weights, weight_specs = zip(*map(_resident, (gpre, win, lng, lnb, ws, bias, wout, gpost)))
    return pl.pallas_call(
        functools.partial(_mixer_kernel, tm=tm, single=single),
        grid=(t // tm,),
        in_specs=[tok(D_MODEL), *weight_specs],
        out_specs=out_specs,
        out_shape=out_shape,
        scratch_shapes=[pltpu.VMEM((tm, D_GATE), BF16)],
        compiler_params=_params(1),
        name="mixer_single" if single else "mixer_chunk",
    )(x, *weights)


def _tail_kernel(*refs, has_oproj):
    if has_oproj:
        x_ref, a_ref, wo_ref, gpm_ref, gpf_ref, wup_ref, wdn_ref, gpo_ref, o_ref = refs
    else:
        x_ref, gpf_ref, wup_ref, wdn_ref, gpo_ref, o_ref = refs
    x = x_ref[...]
    if has_oproj:
        x = x + _rms(_dot(a_ref[...], wo_ref[...]), gpm_ref[...])
    h = _rms(x, gpf_ref[...]).astype(BF16)
    a = jnp.square(jnp.maximum(_dot(h, wup_ref[...]), 0.0)).astype(BF16)
    o_ref[...] = x + _rms(_dot(a, wdn_ref[...]), gpo_ref[...])


def _tail_call(x, attn, wo, gpm, gpf, wup, wdn, gpo, *, tm):
    t = x.shape[0]
    tok = lambda w: pl.BlockSpec((tm, w), lambda i: (i, 0))
    has_oproj = attn is not None
    if has_oproj:
        weights, weight_specs = zip(*map(_resident, (wo, gpm, gpf, wup, wdn, gpo)))
        args = (x, attn, *weights)
        in_specs = [tok(D_MODEL), tok(attn.shape[1]), *weight_specs]
    else:
        weights, weight_specs = zip(*map(_resident, (gpf, wup, wdn, gpo)))
        args = (x, *weights)
        in_specs = [tok(D_MODEL), *weight_specs]
    return pl.pallas_call(
        functools.partial(_tail_kernel, has_oproj=has_oproj),
        grid=(t // tm,),
        in_specs=in_specs,
        out_specs=tok(D_MODEL),
        out_shape=jax.ShapeDtypeStruct((t, D_MODEL), F32),
        compiler_params=_params(1),
        name="tail_oproj" if has_oproj else "tail_ffn",
    )(*args)


def _latent_kernel(*refs, expand):
    if expand:
        (x_ref, gin_ref, wdkv_ref, glat_ref, cos_ref, sin_ref, wuk_ref, wuv_ref,
         c_ref, kr_ref, kfull_ref, v_ref) = refs
    else:
        x_ref, gin_ref, wdkv_ref, glat_ref, cos_ref, sin_ref, c_ref, kr_ref = refs
    h = _rms(x_ref[...], gin_ref[...]).astype(BF16)
    ckr = _dot(h, wdkv_ref[...])
    c = _rms(ckr[:, :KV_LORA], glat_ref[...])
    kr = ckr[:, KV_LORA:LAT_DIM] * cos_ref[...] + ckr[:, LAT_DIM:] * sin_ref[...]
    c_ref[...] = c
    kr_ref[...] = kr
    if expand:
        cb = c.astype(BF16)
        krb = kr.astype(BF16)
        kn = _dot(cb, wuk_ref[...])
        vv = _dot(cb, wuv_ref[...])
        lane = lax.broadcasted_iota(jnp.int32, (cb.shape[0], V_EXT - V_HEAD), 1)
        ones_col = jnp.where(lane == 0, 1.0, 0.0).astype(BF16)
        for hd in range(N_HEADS):
            kfull_ref[0, hd, :, 0:QK_NOPE] = kn[:, hd * QK_NOPE:(hd + 1) * QK_NOPE].astype(BF16)
            kfull_ref[0, hd, :, QK_NOPE:QK_DIM] = krb
            v_ref[0, hd, :, 0:V_HEAD] = vv[:, hd * V_HEAD:(hd + 1) * V_HEAD].astype(BF16)
            v_ref[0, hd, :, V_HEAD:V_EXT] = ones_col


def _latent_call(x, gin, wdkv, glat, cos, sin, wuk, wuv, *, tm, seq, expand):
    t = x.shape[0]
    per_seq = seq // tm
    tok = lambda w: pl.BlockSpec((tm, w), lambda i: (i, 0))
    pos = lambda w: pl.BlockSpec((tm, w), lambda i: (i % per_seq, 0))
    args = [x, gin, wdkv, glat, cos, sin]
    in_specs = [tok(D_MODEL), _const_spec(gin.shape), _const_spec(wdkv.shape),
                _const_spec(glat.shape), pos(QK_ROPE), pos(QK_ROPE)]
    out_shape = [jax.ShapeDtypeStruct((t, KV_LORA), F32), jax.ShapeDtypeStruct((t, QK_ROPE), F32)]
    out_specs = [tok(KV_LORA), tok(QK_ROPE)]
    if expand:
        args += [wuk, wuv]
        in_specs += [_const_spec(wuk.shape), _const_spec(wuv.shape)]
        nb = t // seq
        head_major = lambda w: pl.BlockSpec(
            (1, N_HEADS, tm, w), lambda i: (i // per_seq, 0, i % per_seq, 0))
        out_shape += [jax.ShapeDtypeStruct((nb, N_HEADS, seq, QK_DIM), BF16),
                      jax.ShapeDtypeStruct((nb, N_HEADS, seq, V_EXT), BF16)]
        out_specs += [head_major(QK_DIM), head_major(V_EXT)]
    return pl.pallas_call(
        functools.partial(_latent_kernel, expand=expand),
        grid=(t // tm,),
        in_specs=in_specs,
        out_specs=out_specs,
        out_shape=out_shape,
        compiler_params=_params(1),
        name="latent_expand" if expand else "latent",
    )(*args)


def _q_kernel(*refs, absorb):
    if absorb:
        x_ref, gpre_ref, wdq_ref, gq_ref, wuq_ref, cos_ref, sin_ref, wukt_ref, q_ref = refs
    else:
        x_ref, gpre_ref, wdq_ref, gq_ref, wuq_ref, cos_ref, sin_ref, q_ref = refs
    h = _rms(x_ref[...], gpre_ref[...]).astype(BF16)
    cq = _rms(_dot(h, wdq_ref[...]), gq_ref[...]).astype(BF16)
    q = _dot(cq, wuq_ref[...])
    n_nope = N_HEADS * QK_NOPE
    n_rope = N_HEADS * QK_ROPE
    qn = q[:, :n_nope]
    qr = q[:, n_nope:n_nope + n_rope] * cos_ref[...] + q[:, n_nope + n_rope:] * sin_ref[...]
    scale = SM_SCALE if absorb else SM_SCALE * LOG2E
    for hd in range(N_HEADS):
        qn_h = qn[:, hd * QK_NOPE:(hd + 1) * QK_NOPE]
        qr_h = (qr[:, hd * QK_ROPE:(hd + 1) * QK_ROPE] * scale).astype(BF16)
        if absorb:
            q_lat = _dot(qn_h.astype(BF16), wukt_ref[hd])
            q_ref[hd, :, 0:KV_LORA] = (q_lat * scale).astype(BF16)
            q_ref[hd, :, KV_LORA:LAT_DIM] = qr_h
        else:
            q_ref[0, hd, :, 0:QK_NOPE] = (qn_h * scale).astype(BF16)
            q_ref[0, hd, :, QK_NOPE:QK_DIM] = qr_h


def _q_call(x, gpre, wdq, gq, wuq, cos, sin, wukt, *, tm, seq, absorb):
    t = x.shape[0]
    per_seq = seq // tm
    width = N_HEADS * QK_ROPE
    tok = lambda w: pl.BlockSpec((tm, w), lambda i: (i, 0))
    pos = lambda w: pl.BlockSpec((tm, w), lambda i: (i % per_seq, 0))
    weights, weight_specs = zip(*map(_resident, (gpre, wdq, gq, wuq)))
    args = [x, *weights, cos, sin]
    in_specs = [tok(D_MODEL), *weight_specs, pos(width), pos(width)]
    if absorb:
        args.append(wukt)
        in_specs.append(_const_spec(wukt.shape))
        out_shape = jax.ShapeDtypeStruct((N_HEADS, t, LAT_DIM), BF16)
        out_spec = pl.BlockSpec((N_HEADS, tm, LAT_DIM), lambda i: (0, i, 0))
    else:
        out_shape = jax.ShapeDtypeStruct((t // seq, N_HEADS, seq, QK_DIM), BF16)
        out_spec = pl.BlockSpec((1, N_HEADS, tm, QK_DIM),
                                lambda i: (i // per_seq, 0, i % per_seq, 0))
    return pl.pallas_call(
        functools.partial(_q_kernel, absorb=absorb),
        grid=(t // tm,),
        in_specs=in_specs,
        out_specs=out_spec,
        out_shape=out_shape,
        compiler_params=_params(1),
        name="q_absorb" if absorb else "q_heads",
    )(*args)


def _prompt_attn_kernel(q_ref, k_ref, v_ref, o_ref, m_ref, acc_ref, *, tq, tk, hb):
    i = pl.program_id(2)
    n_full = (i * tq) // tk
    m_ref[...] = jnp.full(m_ref.shape, -jnp.inf, F32)
    acc_ref[...] = jnp.zeros(acc_ref.shape, F32)

    def step(start, mask):
        for hh in range(hb):
            k = k_ref[0, hh, pl.ds(start, tk), :]
            v = v_ref[0, hh, pl.ds(start, tk), :]
            s = _dot_nt(q_ref[0, hh], k)
            if mask is not None:
                s = jnp.where(mask, s, -jnp.inf)
            m_prev = m_ref[hh]
            m_next = jnp.maximum(m_prev, jnp.max(s, axis=-1, keepdims=True))
            alpha = jnp.exp2(m_prev - m_next)
            p = jnp.exp2(s - jnp.concatenate([m_next] * (tk // LANES), axis=1))
            acc_ref[hh] = (acc_ref[hh] * jnp.concatenate([alpha] * (V_EXT // LANES), axis=1)
                           + _dot(p.astype(BF16), v))
            m_ref[hh] = m_next

    def full_step(j, carry):
        step(pl.multiple_of(j * tk, tk), None)
        return carry

    lax.fori_loop(0, n_full, full_step, 0)
    row = lax.broadcasted_iota(jnp.int32, (tq, tk), 0)
    col = lax.broadcasted_iota(jnp.int32, (tq, tk), 1)
    for d in range(tq // tk):
        step(pl.multiple_of(i * tq + d * tk, tk), col + d * tk <= row)
    for hh in range(hb):
        acc = acc_ref[hh]
        o = acc[:, :V_HEAD] / acc[:, V_HEAD:V_HEAD + 1]
        o_ref[0, :, hh * V_HEAD:(hh + 1) * V_HEAD] = o.astype(o_ref.dtype)


def _prompt_attn_call(q, k, v, *, tq, tk, hb):
    nb, nh, seq, _ = q.shape
    return pl.pallas_call(
        functools.partial(_prompt_attn_kernel, tq=tq, tk=tk, hb=hb),
        grid=(nb, nh // hb, seq // tq),
        in_specs=[pl.BlockSpec((1, hb, tq, QK_DIM), lambda b, h, i: (b, h, i, 0)),
                  pl.BlockSpec((1, hb, seq, QK_DIM), lambda b, h, i: (b, h, 0, 0)),
                  pl.BlockSpec((1, hb, seq, V_EXT), lambda b, h, i: (b, h, 0, 0))],
        out_specs=pl.BlockSpec((1, tq, hb * V_HEAD), lambda b, h, i: (b, i, h)),
        out_shape=jax.ShapeDtypeStruct((nb, seq, nh * V_HEAD), BF16),
        scratch_shapes=[pltpu.VMEM((hb, tq, LANES), F32),
                        pltpu.VMEM((hb, tq, V_EXT), F32)],
        compiler_params=_params(3),
        name="prompt_attn",
    )(q, k, v)


def _sample_attn_kernel(pt_ref, q_ref, cnew_ref, krnew_ref, cache_c_ref, cache_krt_ref, o_ref,
                        cbuf_ref, krbuf_ref, sem_ref, kc_ref, krt_ref,
                        *, n_pages, ch, depth):
    b = pl.program_id(0)
    n_rows = pl.num_programs(0)
    n_chunks = n_pages // ch
    page = cbuf_ref.shape[1] // ch

    def chunk_copies(row, chunk):
        slot = chunk % depth
        copies = []
        for r in range(ch):
            pg = pt_ref[row * n_pages + chunk * ch + r]
            copies.append(pltpu.make_async_copy(
                cache_c_ref.at[pg], cbuf_ref.at[slot, pl.ds(r * page, page), :],
                sem_ref.at[slot, r]))
            copies.append(pltpu.make_async_copy(
                cache_krt_ref.at[pg], krbuf_ref.at[slot, r], sem_ref.at[slot, ch + r]))
        return copies

    @pl.when(b == 0)
    def _():
        for chunk in range(depth - 1):
            for cp in chunk_copies(0, chunk):
                cp.start()

    q = q_ref[0]
    q_lat = q[:, :KV_LORA]
    q_rope = q[:, KV_LORA:]

    def scores(chunk):
        ahead = chunk + depth - 1
        if ahead < n_chunks:
            for cp in chunk_copies(b, ahead):
                cp.start()
        else:
            @pl.when(b + 1 < n_rows)
            def _():
                for cp in chunk_copies(b + 1, ahead - n_chunks):
                    cp.start()
        for cp in chunk_copies(b, chunk):
            cp.wait()
        slot = chunk % depth
        half = chunk % 2
        kc_ref[half] = cbuf_ref[slot].astype(BF16)
        for r in range(ch):
            krt_ref[half, :, r * page:(r + 1) * page] = krbuf_ref[slot, r].astype(BF16)
        return _dot_nt(q_lat, kc_ref[half]) + _dot(q_rope, krt_ref[half])

    c_new = cnew_ref[0]
    m = (jnp.sum(q_lat.astype(F32) * c_new, axis=-1, keepdims=True)
         + jnp.sum(q_rope.astype(F32) * krnew_ref[0], axis=-1, keepdims=True))
    l = jnp.ones_like(m)
    acc = jnp.broadcast_to(c_new, (N_HEADS, KV_LORA))

    s_next = scores(0)
    for chunk in range(n_chunks):
        s = s_next
        if chunk + 1 < n_chunks:
            s_next = scores(chunk + 1)
        m_new = jnp.maximum(m, jnp.max(s, axis=-1, keepdims=True))
        alpha = jnp.exp(m - m_new)
        p = jnp.exp(s - m_new)
        l = alpha * l + jnp.sum(p, axis=-1, keepdims=True)
        acc = alpha * acc + _dot(p.astype(BF16), kc_ref[chunk % 2])
        m = m_new

    o_ref[0] = (acc / l).astype(o_ref.dtype)


N_TAIL_FFN_INPUTS = 5
N_SAMPLE_ATTN_INPUTS = 5


def _ffn_sample_attn_kernel(pt_ref, *refs, n_pages, ch, depth):
    tail_in = refs[:N_TAIL_FFN_INPUTS]
    sample_in = refs[N_TAIL_FFN_INPUTS:N_TAIL_FFN_INPUTS + N_SAMPLE_ATTN_INPUTS]
    x_out_ref, o_lat_ref = refs[N_TAIL_FFN_INPUTS + N_SAMPLE_ATTN_INPUTS:][:2]
    scratch = refs[N_TAIL_FFN_INPUTS + N_SAMPLE_ATTN_INPUTS + 2:]
    _tail_kernel(*tail_in, x_out_ref, has_oproj=False)
    _sample_attn_kernel(pt_ref, *sample_in, o_lat_ref, *scratch,
                        n_pages=n_pages, ch=ch, depth=depth)


def _ffn_sample_attn_call(x, gpf, wup, wdn, gpo, page_table, q, c_new, kr_new, cache_c,
                          cache_krt, *, ch, depth):
    t = x.shape[0]
    nb, n_pages = page_table.shape
    page = cache_c.shape[1]
    n_chunks = n_pages // ch
    tm = t // nb
    assert t % nb == 0 and tm % 8 == 0
    assert n_pages % ch == 0 and n_chunks % depth == 0 and depth >= 2
    tok = lambda w: pl.BlockSpec((tm, w), lambda b, pt: (b, 0))
    row = lambda h, w: pl.BlockSpec((1, h, w), lambda b, pt: (b, 0, 0))
    weights, weight_specs = zip(*map(_resident, (gpf, wup, wdn, gpo)))
    return pl.pallas_call(
        functools.partial(_ffn_sample_attn_kernel, n_pages=n_pages, ch=ch, depth=depth),
        grid_spec=pltpu.PrefetchScalarGridSpec(
            num_scalar_prefetch=1,
            grid=(nb,),
            in_specs=[tok(D_MODEL), *weight_specs,
                      row(N_HEADS, LAT_DIM), row(1, KV_LORA), row(1, QK_ROPE),
                      pl.BlockSpec(memory_space=pl.ANY), pl.BlockSpec(memory_space=pl.ANY)],
            out_specs=[tok(D_MODEL), row(N_HEADS, KV_LORA)],
            scratch_shapes=[pltpu.VMEM((depth, ch * page, KV_LORA), F32),
                            pltpu.VMEM((depth, ch, QK_ROPE, page), F32),
                            pltpu.SemaphoreType.DMA((depth, 2 * ch)),
                            pltpu.VMEM((2, ch * page, KV_LORA), BF16),
                            pltpu.VMEM((2, QK_ROPE, ch * page), BF16)],
        ),
        out_shape=[jax.ShapeDtypeStruct((t, D_MODEL), F32),
                   jax.ShapeDtypeStruct((nb, N_HEADS, KV_LORA), BF16)],
        compiler_params=_params(1),
        name="ffn_sample_attn",
    )(page_table.reshape(-1), x, *weights, q, c_new.reshape(nb, 1, KV_LORA),
      kr_new.reshape(nb, 1, QK_ROPE), cache_c, cache_krt)


def _uv_kernel(o_ref, wuv_ref, out_ref):
    for hd in range(N_HEADS):
        out_ref[:, hd * V_HEAD:(hd + 1) * V_HEAD] = _dot(o_ref[hd], wuv_ref[hd]).astype(out_ref.dtype)


def _uv_call(o_lat, wuv_heads):
    nh, t, _ = o_lat.shape
    return pl.pallas_call(
        _uv_kernel,
        out_shape=jax.ShapeDtypeStruct((t, nh * V_HEAD), BF16),
        compiler_params=pltpu.CompilerParams(vmem_limit_bytes=VMEM_LIMIT_BYTES),
        name="uv_proj",
    )(o_lat, wuv_heads)


def _rope_tables(pos, reps):
    half = QK_ROPE // 2
    inv = ROPE_BASE ** (-jnp.arange(half, dtype=F32) / half)
    ang = pos.astype(F32)[:, None] * inv[None, :]
    cos = jnp.cos(ang)
    sin = jnp.sin(ang)
    cos64 = jnp.concatenate([cos, cos], axis=1)
    sin64 = jnp.concatenate([-sin, sin], axis=1)
    return cos64, sin64, jnp.tile(cos64, (1, reps)), jnp.tile(sin64, (1, reps))


def _row(v):
    return v.reshape(1, -1).astype(F32)


@jax.jit
def _forward(x_prompt, x_sample, cache_kv_latent, cache_k_rope, page_table,
             norm_pre_mix, norm_post_mix, norm_pre_ffn, norm_post_ffn,
             a_w_in, a_ln_g, a_ln_b, a_w_s, a_b_s, a_w_out,
             kv_norm_in, w_dkv, kv_latent_norm, w_uk, w_uv,
             b_w_dq, b_q_norm, b_w_uq, b_w_o, ffn_w_up, ffn_w_down):
    nb, seq, _ = x_prompt.shape
    db, dseq, _ = x_sample.shape
    n_pages = page_table.shape[1]
    n_past = n_pages * cache_kv_latent.shape[1]

    bf = lambda w: w.astype(BF16)
    w_in = bf(a_w_in)
    w_out = bf(a_w_out)
    w_up = bf(ffn_w_up)
    w_down = bf(ffn_w_down)
    half = QK_ROPE // 2
    swap_halves = lambda w: jnp.concatenate([w[..., half:], w[..., :half]], axis=-1)
    w_dkv_b = bf(jnp.concatenate([w_dkv, swap_halves(w_dkv[:, KV_LORA:])], axis=1))
    w_uk2d = bf(w_uk.reshape(KV_LORA, N_HEADS * QK_NOPE))
    w_uv2d = bf(w_uv.reshape(KV_LORA, N_HEADS * V_HEAD))
    w_ukt = bf(jnp.transpose(w_uk, (1, 2, 0)))
    w_uv_heads = bf(jnp.transpose(w_uv, (1, 0, 2)))
    w_dq = bf(b_w_dq)
    w_o = bf(b_w_o)
    cache_krt = jnp.transpose(cache_k_rope, (0, 2, 1))
    w_uq4 = b_w_uq.reshape(N_B_LAYERS, Q_LORA, N_HEADS, QK_DIM)
    w_uq_rope = w_uq4[..., QK_NOPE:]
    w_uq = bf(jnp.concatenate(
        [w_uq4[..., :QK_NOPE].reshape(N_B_LAYERS, Q_LORA, N_HEADS * QK_NOPE),
         w_uq_rope.reshape(N_B_LAYERS, Q_LORA, N_HEADS * QK_ROPE),
         swap_halves(w_uq_rope).reshape(N_B_LAYERS, Q_LORA, N_HEADS * QK_ROPE)], axis=-1))

    bias_chunk = jnp.repeat(jnp.transpose(a_b_s, (0, 2, 1)), GROUP_DIM, axis=2)
    bias_single = jnp.repeat(a_b_s[:, :, 0], GROUP_DIM, axis=1)[:, None, :]
    w_single = jnp.repeat(a_w_s[:, :, 0, 0], GROUP_DIM, axis=1)[:, None, :]

    def mixer(x, l, *, tm, single):
        return _mixer_call(
            x, _row(norm_pre_mix[l]), (w_in, l), _row(a_ln_g[l]), _row(a_ln_b[l]),
            (w_single if single else a_w_s, l), (bias_single if single else bias_chunk, l),
            (w_out, l), _row(norm_post_mix[l]), tm=tm, single=single)

    def ffn_weights(l):
        return _row(norm_pre_ffn[l]), (w_up, l), (w_down, l), _row(norm_post_ffn[l])

    def tail(x, l, attn, *, tm):
        if attn is None:
            return _tail_call(x, None, None, None, *ffn_weights(l), tm=tm)
        return _tail_call(x, attn, (w_o, l - N_A_LAYERS), _row(norm_post_mix[l]),
                          *ffn_weights(l), tm=tm)

    def queries(x, l, tables, *, tm, seq_len, absorb):
        j = l - N_A_LAYERS
        return _q_call(x, _row(norm_pre_mix[l]), (w_dq, j), _row(b_q_norm[j]), (w_uq, j),
                       tables[2], tables[3], w_ukt, tm=tm, seq=seq_len, absorb=absorb)

    def latent(x, tables, *, tm, seq_len, expand):
        return _latent_call(x, _row(kv_norm_in), w_dkv_b, _row(kv_latent_norm), tables[0],
                            tables[1], w_uk2d, w_uv2d, tm=tm, seq=seq_len, expand=expand)

    assert N_B_LAYERS <= N_A_LAYERS
    ts = db * dseq
    tp = PROMPT_TOKEN_TILE
    tables_s = _rope_tables(jnp.full((ts,), n_past, dtype=jnp.int32), N_HEADS)
    tables_p = _rope_tables(jnp.arange(seq, dtype=jnp.int32), N_HEADS)

    xs = x_sample.reshape(ts, D_MODEL)
    v_rows = []
    for l in range(N_A_LAYERS):
        xs, v = mixer(xs, l, tm=ts, single=True)
        v_rows.append(v)
        xs = tail(xs, l, None, tm=ts)
    c_s, kr_s = latent(xs, tables_s, tm=ts, seq_len=ts, expand=False)
    q_s = queries(xs, N_A_LAYERS, tables_s, tm=ts, seq_len=ts, absorb=True)

    xp = x_prompt.reshape(nb * seq, D_MODEL)
    for l in range(N_A_LAYERS):
        (xp,) = mixer(xp, l, tm=tp, single=False)
        if l >= N_B_LAYERS:
            xp = tail(xp, l, None, tm=tp)
            continue
        lb = N_A_LAYERS + l
        xp, o_lat = _ffn_sample_attn_call(
            xp, *ffn_weights(l), page_table, jnp.transpose(q_s, (1, 0, 2)), c_s, kr_s,
            cache_kv_latent, cache_krt, ch=PAGES_PER_CHUNK, depth=PAGE_RING_DEPTH)
        attn_s = _uv_call(jnp.transpose(o_lat, (1, 0, 2)), w_uv_heads)
        xs = tail(xs, lb, attn_s, tm=ts)
        if l + 1 < N_B_LAYERS:
            q_s = queries(xs, lb + 1, tables_s, tm=ts, seq_len=ts, absorb=True)

    c_p, kr_p, kfull, vfull = latent(xp, tables_p, tm=tp, seq_len=seq, expand=True)
    for l in range(N_A_LAYERS, N_A_LAYERS + N_B_LAYERS):
        q_p = queries(xp, l, tables_p, tm=tp, seq_len=seq, absorb=False)
        attn_p = _prompt_attn_call(q_p, kfull, vfull, tq=ATTN_Q_TILE, tk=ATTN_KV_TILE,
                                   hb=ATTN_HEADS_PER_STEP)
        xp = tail(xp, l, attn_p.reshape(nb * seq, N_HEADS * V_HEAD), tm=tp)

    gate_v = jnp.stack(v_rows, axis=0).reshape(N_A_LAYERS, db, dseq, D_GATE)
    return (xp.reshape(nb, seq, D_MODEL), xs.reshape(db, dseq, D_MODEL),
            c_p.reshape(nb, seq, KV_LORA), kr_p.reshape(nb, seq, QK_ROPE),
            c_s.reshape(db, dseq, KV_LORA), kr_s.reshape(db, dseq, QK_ROPE), gate_v)


def kernel(x_prompt, x_sample, cache_kv_latent, cache_k_rope, page_table, norm_pre_mix, norm_post_mix, norm_pre_ffn, norm_post_ffn, a_w_in, a_ln_g, a_ln_b, a_w_s, a_b_s, a_w_out, kv_norm_in, w_dkv, kv_latent_norm, w_uk, w_uv, b_w_dq, b_q_norm, b_w_uq, b_w_o, ffn_w_up, ffn_w_down):
    assert x_sample.shape[1] == 1, "sample path assumes one new token per row"
    return _forward(x_prompt, x_sample, cache_kv_latent, cache_k_rope, page_table,
                    norm_pre_mix, norm_post_mix, norm_pre_ffn, norm_post_ffn,
                    a_w_in, a_ln_g, a_ln_b, a_w_s, a_b_s, a_w_out,
                    kv_norm_in, w_dkv, kv_latent_norm, w_uk, w_uv,
                    b_w_dq, b_q_norm, b_w_uq, b_w_o, ffn_w_up, ffn_w_down)
```

```python
import functools

import jax
import jax.numpy as jnp
from jax import lax
from jax.experimental import pallas as pl
from jax.experimental.pallas import tpu as pltpu

D_MODEL = 1024
CHUNK = 128
D_GATE = 2 * D_MODEL
N_GROUPS = 8
GROUP_DIM = D_GATE // N_GROUPS
D_FF = 4 * D_MODEL
N_HEADS = 16
QK_NOPE = 128
QK_ROPE = 64
QK_DIM = QK_NOPE + QK_ROPE
V_HEAD = 128
Q_LORA = 256
KV_LORA = 512
LAT_DIM = KV_LORA + QK_ROPE
ROPE_BASE = 10000.0
EPS = 1e-6
SM_SCALE = QK_DIM ** -0.5
N_A_LAYERS = 2
N_B_LAYERS = 2

F32 = jnp.float32
BF16 = jnp.bfloat16

VMEM_LIMIT_BYTES = 56 * 1024 * 1024

LANES = 128
V_EXT = 2 * LANES
LOG2E = 1.4426950408889634

PROMPT_TOKEN_TILE = 512
ATTN_Q_TILE = 512
ATTN_KV_TILE = 512
ATTN_HEADS_PER_STEP = 8
PAGES_PER_CHUNK = 16
PAGE_RING_DEPTH = 4
SAMPLE_ROWS_PER_STEP = 2


def _dot(a, b):
    return jnp.dot(a, b, preferred_element_type=F32)


def _dot_nt(a, b):
    return lax.dot_general(a, b, (((1,), (1,)), ((), ())), preferred_element_type=F32)


def _rms(x, g):
    return x * lax.rsqrt(jnp.mean(x * x, axis=-1, keepdims=True) + EPS) * g


def _gelu(z):
    return 0.5 * z * (1.0 + lax.erf(z * (0.5 ** 0.5)))


def _const_spec(shape):
    n = len(shape)
    return pl.BlockSpec(shape, lambda *_: (0,) * n, pipeline_mode=pl.Buffered(1))


def _resident(w):
    if isinstance(w, tuple):
        arr, layer = w
        n = arr.ndim - 1
        return arr, pl.BlockSpec((None,) + arr.shape[1:], lambda *_: (layer,) + (0,) * n,
                                 pipeline_mode=pl.Buffered(1))
    return w, _const_spec(w.shape)


def _params(n_grid):
    return pltpu.CompilerParams(
        dimension_semantics=("arbitrary",) * n_grid,
        vmem_limit_bytes=VMEM_LIMIT_BYTES,
    )


def _mixer_kernel(x_ref, gpre_ref, win_ref, lng_ref, lnb_ref, ws_ref, bias_ref, wout_ref,
                  gpost_ref, *rest, tm, single):
    if single:
        o_ref, v_ref, gated_ref = rest
    else:
        o_ref, gated_ref = rest
    x = x_ref[...]
    h = _rms(x, gpre_ref[...]).astype(BF16)
    v = _gelu(_dot(h, win_ref[:, D_GATE:]))
    u = _gelu(_dot(h, win_ref[:, :D_GATE]))
    mu = jnp.mean(v, axis=-1, keepdims=True)
    vc = v - mu
    v = vc * lax.rsqrt(jnp.mean(vc * vc, axis=-1, keepdims=True) + EPS) * lng_ref[...] + lnb_ref[...]
    if single:
        v_ref[...] = v
        gated_ref[...] = (u * (v * ws_ref[...] + bias_ref[...])).astype(BF16)
    else:
        vb = v.astype(BF16)
        row = lax.broadcasted_iota(jnp.int32, (CHUNK, CHUNK), 0)
        col = lax.broadcasted_iota(jnp.int32, (CHUNK, CHUNK), 1)
        causal = row >= col
        for g in range(N_GROUPS):
            wg = jnp.where(causal, ws_ref[g], 0.0).astype(BF16)
            cols = slice(g * GROUP_DIM, (g + 1) * GROUP_DIM)
            for c in range(tm // CHUNK):
                rows = slice(c * CHUNK, (c + 1) * CHUNK)
                s = _dot(wg, vb[rows, cols]) + bias_ref[:, cols]
                gated_ref[rows, cols] = (u[rows, cols] * s).astype(BF16)
    m = _dot(gated_ref[...], wout_ref[...])
    o_ref[...] = x + _rms(m, gpost_ref[...])


def _mixer_call(x, gpre, win, lng, lnb, ws, bias, wout, gpost, *, tm, single):
    t = x.shape[0]
    tok = lambda w: pl.BlockSpec((tm, w), lambda i: (i, 0))
    out_shape = [jax.ShapeDtypeStruct((t, D_MODEL), F32)]
    out_specs = [tok(D_MODEL)]
    if single:
        out_shape.append(jax.ShapeDtypeStruct((t, D_GATE), F32))
        out_specs.append(tok(D_GATE))
    weights, weight_specs = zip(*map(_resident, (gpre, win, lng, lnb, ws, bias, wout, gpost)))
    return pl.pallas_call(
        functools.partial(_mixer_kernel, tm=tm, single=single),
        grid=(t // tm,),
        in_specs=[tok(D_MODEL), *weight_specs],
        out_specs=out_specs,
        out_shape=out_shape,
        scratch_shapes=[pltpu.VMEM((tm, D_GATE), BF16)],
        compiler_params=_params(1),
        name="mixer_single" if single else "mixer_chunk",
    )(x, *weights)


def _tail_kernel(*refs, has_oproj):
    if has_oproj:
        x_ref, a_ref, wo_ref, gpm_ref, gpf_ref, wup_ref, wdn_ref, gpo_ref, o_ref = refs
    else:
        x_ref, gpf_ref, wup_ref, wdn_ref, gpo_ref, o_ref = refs
    x = x_ref[...]
    if has_oproj:
        x = x + _rms(_dot(a_ref[...], wo_ref[...]), gpm_ref[...])
    h = _rms(x, gpf_ref[...]).astype(BF16)
    a = jnp.square(jnp.maximum(_dot(h, wup_ref[...]), 0.0)).astype(BF16)
    o_ref[...] = x + _rms(_dot(a, wdn_ref[...]), gpo_ref[...])


def _tail_call(x, attn, wo, gpm, gpf, wup, wdn, gpo, *, tm):
    t = x.shape[0]
    tok = lambda w: pl.BlockSpec((tm, w), lambda i: (i, 0))
    has_oproj = attn is not None
    if has_oproj:
        weights, weight_specs = zip(*map(_resident, (wo, gpm, gpf, wup, wdn, gpo)))
        args = (x, attn, *weights)
        in_specs = [tok(D_MODEL), tok(attn.shape[1]), *weight_specs]
    else:
        weights, weight_specs = zip(*map(_resident, (gpf, wup, wdn, gpo)))
        args = (x, *weights)
        in_specs = [tok(D_MODEL), *weight_specs]
    return pl.pallas_call(
        functools.partial(_tail_kernel, has_oproj=has_oproj),
        grid=(t // tm,),
        in_specs=in_specs,
        out_specs=tok(D_MODEL),
        out_shape=jax.ShapeDtypeStruct((t, D_MODEL), F32),
        compiler_params=_params(1),
        name="tail_oproj" if has_oproj else "tail_ffn",
    )(*args)


def _latent_kernel(*refs, expand):
    if expand:
        (x_ref, gin_ref, wdkv_ref, glat_ref, cos_ref, sin_ref, wuk_ref, wuv_ref,
         c_ref, kr_ref, kfull_ref, v_ref) = refs
    else:
        x_ref, gin_ref, wdkv_ref, glat_ref, cos_ref, sin_ref, c_ref, kr_ref = refs
    h = _rms(x_ref[...], gin_ref[...]).astype(BF16)
    ckr = _dot(h, wdkv_ref[...])
    c = _rms(ckr[:, :KV_LORA], glat_ref[...])
    kr = ckr[:, KV_LORA:LAT_DIM] * cos_ref[...] + ckr[:, LAT_DIM:] * sin_ref[...]
    c_ref[...] = c
    kr_ref[...] = kr
    if expand:
        cb = c.astype(BF16)
        krb = kr.astype(BF16)
        kn = _dot(cb, wuk_ref[...])
        vv = _dot(cb, wuv_ref[...])
        lane = lax.broadcasted_iota(jnp.int32, (cb.shape[0], V_EXT - V_HEAD), 1)
        ones_col = jnp.where(lane == 0, 1.0, 0.0).astype(BF16)
        for hd in range(N_HEADS):
            kfull_ref[0, hd, :, 0:QK_NOPE] = kn[:, hd * QK_NOPE:(hd + 1) * QK_NOPE].astype(BF16)
            kfull_ref[0, hd, :, QK_NOPE:QK_DIM] = krb
            v_ref[0, hd, :, 0:V_HEAD] = vv[:, hd * V_HEAD:(hd + 1) * V_HEAD].astype(BF16)
            v_ref[0, hd, :, V_HEAD:V_EXT] = ones_col


def _latent_call(x, gin, wdkv, glat, cos, sin, wuk, wuv, *, tm, seq, expand):
    t = x.shape[0]
    per_seq = seq // tm
    tok = lambda w: pl.BlockSpec((tm, w), lambda i: (i, 0))
    pos = lambda w: pl.BlockSpec((tm, w), lambda i: (i % per_seq, 0))
    args = [x, gin, wdkv, glat, cos, sin]
    in_specs = [tok(D_MODEL), _const_spec(gin.shape), _const_spec(wdkv.shape),
                _const_spec(glat.shape), pos(QK_ROPE), pos(QK_ROPE)]
    out_shape = [jax.ShapeDtypeStruct((t, KV_LORA), F32), jax.ShapeDtypeStruct((t, QK_ROPE), F32)]
    out_specs = [tok(KV_LORA), tok(QK_ROPE)]
    if expand:
        args += [wuk, wuv]
        in_specs += [_const_spec(wuk.shape), _const_spec(wuv.shape)]
        nb = t // seq
        head_major = lambda w: pl.BlockSpec(
            (1, N_HEADS, tm, w), lambda i: (i // per_seq, 0, i % per_seq, 0))
        out_shape += [jax.ShapeDtypeStruct((nb, N_HEADS, seq, QK_DIM), BF16),
                      jax.ShapeDtypeStruct((nb, N_HEADS, seq, V_EXT), BF16)]
        out_specs += [head_major(QK_DIM), head_major(V_EXT)]
    return pl.pallas_call(
        functools.partial(_latent_kernel, expand=expand),
        grid=(t // tm,),
        in_specs=in_specs,
        out_specs=out_specs,
        out_shape=out_shape,
        compiler_params=_params(1),
        name="latent_expand" if expand else "latent",
    )(*args)


def _q_kernel(*refs, absorb):
    if absorb:
        x_ref, gpre_ref, wdq_ref, gq_ref, wuq_ref, cos_ref, sin_ref, wukt_ref, q_ref = refs
    else:
        x_ref, gpre_ref, wdq_ref, gq_ref, wuq_ref, cos_ref, sin_ref, q_ref = refs
    h = _rms(x_ref[...], gpre_ref[...]).astype(BF16)
    cq = _rms(_dot(h, wdq_ref[...]), gq_ref[...]).astype(BF16)
    q = _dot(cq, wuq_ref[...])
    n_nope = N_HEADS * QK_NOPE
    n_rope = N_HEADS * QK_ROPE
    qn = q[:, :n_nope]
    qr = q[:, n_nope:n_nope + n_rope] * cos_ref[...] + q[:, n_nope + n_rope:] * sin_ref[...]
    scale = SM_SCALE if absorb else SM_SCALE * LOG2E
    for hd in range(N_HEADS):
        qn_h = qn[:, hd * QK_NOPE:(hd + 1) * QK_NOPE]
        qr_h = (qr[:, hd * QK_ROPE:(hd + 1) * QK_ROPE] * scale).astype(BF16)
        if absorb:
            q_lat = _dot(qn_h.astype(BF16), wukt_ref[hd])
            q_ref[hd, :, 0:KV_LORA] = (q_lat * scale).astype(BF16)
            q_ref[hd, :, KV_LORA:LAT_DIM] = qr_h
        else:
            q_ref[0, hd, :, 0:QK_NOPE] = (qn_h * scale).astype(BF16)
            q_ref[0, hd, :, QK_NOPE:QK_DIM] = qr_h


def _q_call(x, gpre, wdq, gq, wuq, cos, sin, wukt, *, tm, seq, absorb):
    t = x.shape[0]
    per_seq = seq // tm
    width = N_HEADS * QK_ROPE
    tok = lambda w: pl.BlockSpec((tm, w), lambda i: (i, 0))
    pos = lambda w: pl.BlockSpec((tm, w), lambda i: (i % per_seq, 0))
    weights, weight_specs = zip(*map(_resident, (gpre, wdq, gq, wuq)))
    args = [x, *weights, cos, sin]
    in_specs = [tok(D_MODEL), *weight_specs, pos(width), pos(width)]
    if absorb:
        args.append(wukt)
        in_specs.append(_const_spec(wukt.shape))
        out_shape = jax.ShapeDtypeStruct((N_HEADS, t, LAT_DIM), BF16)
        out_spec = pl.BlockSpec((N_HEADS, tm, LAT_DIM), lambda i: (0, i, 0))
    else:
        out_shape = jax.ShapeDtypeStruct((t // seq, N_HEADS, seq, QK_DIM), BF16)
        out_spec = pl.BlockSpec((1, N_HEADS, tm, QK_DIM),
                                lambda i: (i // per_seq, 0, i % per_seq, 0))
    return pl.pallas_call(
        functools.partial(_q_kernel, absorb=absorb),
        grid=(t // tm,),
        in_specs=in_specs,
        out_specs=out_spec,
        out_shape=out_shape,
        compiler_params=_params(1),
        name="q_absorb" if absorb else "q_heads",
    )(*args)


def _prompt_attn_kernel(q_ref, k_ref, v_ref, o_ref, m_ref, acc_ref, *, tq, tk, hb):
    i = pl.program_id(2)
    n_full = (i * tq) // tk
    m_ref[...] = jnp.full(m_ref.shape, -jnp.inf, F32)
    acc_ref[...] = jnp.zeros(acc_ref.shape, F32)

    def step(start, mask):
        def scores(hh):
            return _dot_nt(q_ref[0, hh], k_ref[0, hh, pl.ds(start, tk), :])

        s_next = scores(0)
        for hh in range(hb):
            s = s_next
            if hh + 1 < hb:
                s_next = scores(hh + 1)
            v = v_ref[0, hh, pl.ds(start, tk), :]
            if mask is not None:
                s = jnp.where(mask, s, -jnp.inf)
            m_prev = m_ref[hh]
            m_next = jnp.maximum(m_prev, jnp.max(s, axis=-1, keepdims=True))
            alpha = jnp.exp2(m_prev - m_next)
            p = jnp.exp2(s - jnp.concatenate([m_next] * (tk // LANES), axis=1))
            acc_ref[hh] = (acc_ref[hh] * jnp.concatenate([alpha] * (V_EXT // LANES), axis=1)
                           + _dot(p.astype(BF16), v))
            m_ref[hh] = m_next

    def full_step(j, carry):
        step(pl.multiple_of(j * tk, tk), None)
        return carry

    lax.fori_loop(0, n_full, full_step, 0)
    row = lax.broadcasted_iota(jnp.int32, (tq, tk), 0)
    col = lax.broadcasted_iota(jnp.int32, (tq, tk), 1)
    for d in range(tq // tk):
        step(pl.multiple_of(i * tq + d * tk, tk), col + d * tk <= row)
    for hh in range(hb):
        acc = acc_ref[hh]
        o = acc[:, :V_HEAD] / acc[:, V_HEAD:V_HEAD + 1]
        o_ref[0, :, hh * V_HEAD:(hh + 1) * V_HEAD] = o.astype(o_ref.dtype)


def _prompt_attn_call(q, k, v, *, tq, tk, hb):
    nb, nh, seq, _ = q.shape
    assert tq % tk == 0 and seq % tq == 0 and nh % hb == 0
    return pl.pallas_call(
        functools.partial(_prompt_attn_kernel, tq=tq, tk=tk, hb=hb),
        grid=(nb, nh // hb, seq // tq),
        in_specs=[pl.BlockSpec((1, hb, tq, QK_DIM), lambda b, h, i: (b, h, i, 0)),
                  pl.BlockSpec((1, hb, seq, QK_DIM), lambda b, h, i: (b, h, 0, 0)),
                  pl.BlockSpec((1, hb, seq, V_EXT), lambda b, h, i: (b, h, 0, 0))],
        out_specs=pl.BlockSpec((1, tq, hb * V_HEAD), lambda b, h, i: (b, i, h)),
        out_shape=jax.ShapeDtypeStruct((nb, seq, nh * V_HEAD), BF16),
        scratch_shapes=[pltpu.VMEM((hb, tq, LANES), F32),
                        pltpu.VMEM((hb, tq, V_EXT), F32)],
        compiler_params=_params(3),
        name="prompt_attn",
    )(q, k, v)


def _sample_attn_kernel(pt_ref, q_ref, cnew_ref, krnew_ref, cache_c_ref, cache_krt_ref, o_ref,
                        cbuf_ref, krbuf_ref, sem_ref, kc_ref, krt_ref,
                        *, n_pages, ch, depth, rows):
    n_rows = pl.num_programs(0) * rows
    n_chunks = n_pages // ch
    page = cbuf_ref.shape[1] // ch

    def chunk_copies(row, chunk):
        slot = chunk % depth
        copies = []
        for r in range(ch):
            pg = pt_ref[row * n_pages + chunk * ch + r]
            copies.append(pltpu.make_async_copy(
                cache_c_ref.at[pg], cbuf_ref.at[slot, pl.ds(r * page, page), :],
                sem_ref.at[slot, r]))
            copies.append(pltpu.make_async_copy(
                cache_krt_ref.at[pg], krbuf_ref.at[slot, r], sem_ref.at[slot, ch + r]))
        return copies

    @pl.when(pl.program_id(0) == 0)
    def _():
        for chunk in range(depth - 1):
            for cp in chunk_copies(0, chunk):
                cp.start()

    def attend(b, q, c_new, kr_new):
        q_lat = q[:, :KV_LORA]
        q_rope = q[:, KV_LORA:]

        def scores(chunk):
            ahead = chunk + depth - 1
            if ahead < n_chunks:
                for cp in chunk_copies(b, ahead):
                    cp.start()
            else:
                @pl.when(b + 1 < n_rows)
                def _():
                    for cp in chunk_copies(b + 1, ahead - n_chunks):
                        cp.start()
            for cp in chunk_copies(b, chunk):
                cp.wait()
            slot = chunk % depth
            half = chunk % 2
            kc_ref[half] = cbuf_ref[slot].astype(BF16)
            for r in range(ch):
                krt_ref[half, :, r * page:(r + 1) * page] = krbuf_ref[slot, r].astype(BF16)
            return _dot_nt(q_lat, kc_ref[half]) + _dot(q_rope, krt_ref[half])

        m = (jnp.sum(q_lat.astype(F32) * c_new, axis=-1, keepdims=True)
             + jnp.sum(q_rope.astype(F32) * kr_new, axis=-1, keepdims=True))
        l = jnp.ones_like(m)
        acc = jnp.broadcast_to(c_new, (N_HEADS, KV_LORA))

        s_next = scores(0)
        for chunk in range(n_chunks):
            s = s_next
            if chunk + 1 < n_chunks:
                s_next = scores(chunk + 1)
            m_new = jnp.maximum(m, jnp.max(s, axis=-1, keepdims=True))
            alpha = jnp.exp(m - m_new)
            p = jnp.exp(s - m_new)
            l = alpha * l + jnp.sum(p, axis=-1, keepdims=True)
            acc = alpha * acc + _dot(p.astype(BF16), kc_ref[chunk % 2])
            m = m_new
        return acc / l

    for r in range(rows):
        o = attend(pl.program_id(0) * rows + r, q_ref[r], cnew_ref[r], krnew_ref[r])
        o_ref[r] = o.astype(o_ref.dtype)


N_TAIL_FFN_INPUTS = 5
N_SAMPLE_ATTN_INPUTS = 5


def _ffn_sample_attn_kernel(pt_ref, *refs, n_pages, ch, depth, rows):
    tail_in = refs[:N_TAIL_FFN_INPUTS]
    sample_in = refs[N_TAIL_FFN_INPUTS:N_TAIL_FFN_INPUTS + N_SAMPLE_ATTN_INPUTS]
    x_out_ref, o_lat_ref = refs[N_TAIL_FFN_INPUTS + N_SAMPLE_ATTN_INPUTS:][:2]
    scratch = refs[N_TAIL_FFN_INPUTS + N_SAMPLE_ATTN_INPUTS + 2:]
    _tail_kernel(*tail_in, x_out_ref, has_oproj=False)
    _sample_attn_kernel(pt_ref, *sample_in, o_lat_ref, *scratch,
                        n_pages=n_pages, ch=ch, depth=depth, rows=rows)


def _ffn_sample_attn_call(x, gpf, wup, wdn, gpo, page_table, q, c_new, kr_new, cache_c,
                          cache_krt, *, ch, depth, rows):
    t = x.shape[0]
    nb, n_pages = page_table.shape
    page = cache_c.shape[1]
    n_chunks = n_pages // ch
    n_steps = nb // rows
    tm = t // n_steps
    assert nb % rows == 0 and t % n_steps == 0 and tm % 8 == 0
    assert n_pages % ch == 0 and n_chunks % depth == 0 and depth >= 2
    tok = lambda w: pl.BlockSpec((tm, w), lambda b, pt: (b, 0))
    row = lambda h, w: pl.BlockSpec((rows, h, w), lambda b, pt: (b, 0, 0))
    weights, weight_specs = zip(*map(_resident, (gpf, wup, wdn, gpo)))
    return pl.pallas_call(
        functools.partial(_ffn_sample_attn_kernel, n_pages=n_pages, ch=ch, depth=depth,
                          rows=rows),
        grid_spec=pltpu.PrefetchScalarGridSpec(
            num_scalar_prefetch=1,
            grid=(n_steps,),
            in_specs=[tok(D_MODEL), *weight_specs,
                      row(N_HEADS, LAT_DIM), row(1, KV_LORA), row(1, QK_ROPE),
                      pl.BlockSpec(memory_space=pl.ANY), pl.BlockSpec(memory_space=pl.ANY)],
            out_specs=[tok(D_MODEL), row(N_HEADS, KV_LORA)],
            scratch_shapes=[pltpu.VMEM((depth, ch * page, KV_LORA), F32),
                            pltpu.VMEM((depth, ch, QK_ROPE, page), F32),
                            pltpu.SemaphoreType.DMA((depth, 2 * ch)),
                            pltpu.VMEM((2, ch * page, KV_LORA), BF16),
                            pltpu.VMEM((2, QK_ROPE, ch * page), BF16)],
        ),
        out_shape=[jax.ShapeDtypeStruct((t, D_MODEL), F32),
                   jax.ShapeDtypeStruct((nb, N_HEADS, KV_LORA), BF16)],
        compiler_params=_params(1),
        name="ffn_sample_attn",
    )(page_table.reshape(-1), x, *weights, q, c_new.reshape(nb, 1, KV_LORA),
      kr_new.reshape(nb, 1, QK_ROPE), cache_c, cache_krt)


def _uv_kernel(o_ref, wuv_ref, out_ref):
    for hd in range(N_HEADS):
        out_ref[:, hd * V_HEAD:(hd + 1) * V_HEAD] = _dot(o_ref[hd], wuv_ref[hd]).astype(out_ref.dtype)


def _uv_call(o_lat, wuv_heads):
    nh, t, _ = o_lat.shape
    return pl.pallas_call(
        _uv_kernel,
        out_shape=jax.ShapeDtypeStruct((t, nh * V_HEAD), BF16),
        compiler_params=pltpu.CompilerParams(vmem_limit_bytes=VMEM_LIMIT_BYTES),
        name="uv_proj",
    )(o_lat, wuv_heads)


def _rope_tables(pos, reps):
    half = QK_ROPE // 2
    inv = ROPE_BASE ** (-jnp.arange(half, dtype=F32) / half)
    ang = pos.astype(F32)[:, None] * inv[None, :]
    cos = jnp.cos(ang)
    sin = jnp.sin(ang)
    cos64 = jnp.concatenate([cos, cos], axis=1)
    sin64 = jnp.concatenate([-sin, sin], axis=1)
    return cos64, sin64, jnp.tile(cos64, (1, reps)), jnp.tile(sin64, (1, reps))


def _row(v):
    return v.reshape(1, -1).astype(F32)


@jax.jit
def _forward(x_prompt, x_sample, cache_kv_latent, cache_k_rope, page_table,
             norm_pre_mix, norm_post_mix, norm_pre_ffn, norm_post_ffn,
             a_w_in, a_ln_g, a_ln_b, a_w_s, a_b_s, a_w_out,
             kv_norm_in, w_dkv, kv_latent_norm, w_uk, w_uv,
             b_w_dq, b_q_norm, b_w_uq, b_w_o, ffn_w_up, ffn_w_down):
    nb, seq, _ = x_prompt.shape
    db, dseq, _ = x_sample.shape
    n_pages = page_table.shape[1]
    n_past = n_pages * cache_kv_latent.shape[1]

    bf = lambda w: w.astype(BF16)
    w_in = bf(a_w_in)
    w_out = bf(a_w_out)
    w_up = bf(ffn_w_up)
    w_down = bf(ffn_w_down)
    half = QK_ROPE // 2
    swap_halves = lambda w: jnp.concatenate([w[..., half:], w[..., :half]], axis=-1)
    w_dkv_b = bf(jnp.concatenate([w_dkv, swap_halves(w_dkv[:, KV_LORA:])], axis=1))
    w_uk2d = bf(w_uk.reshape(KV_LORA, N_HEADS * QK_NOPE))
    w_uv2d = bf(w_uv.reshape(KV_LORA, N_HEADS * V_HEAD))
    w_ukt = bf(jnp.transpose(w_uk, (1, 2, 0)))
    w_uv_heads = bf(jnp.transpose(w_uv, (1, 0, 2)))
    w_dq = bf(b_w_dq)
    w_o = bf(b_w_o)
    cache_krt = jnp.transpose(cache_k_rope, (0, 2, 1))
    w_uq4 = b_w_uq.reshape(N_B_LAYERS, Q_LORA, N_HEADS, QK_DIM)
    w_uq_rope = w_uq4[..., QK_NOPE:]
    w_uq = bf(jnp.concatenate(
        [w_uq4[..., :QK_NOPE].reshape(N_B_LAYERS, Q_LORA, N_HEADS * QK_NOPE),
         w_uq_rope.reshape(N_B_LAYERS, Q_LORA, N_HEADS * QK_ROPE),
         swap_halves(w_uq_rope).reshape(N_B_LAYERS, Q_LORA, N_HEADS * QK_ROPE)], axis=-1))

    bias_chunk = jnp.repeat(jnp.transpose(a_b_s, (0, 2, 1)), GROUP_DIM, axis=2)
    bias_single = jnp.repeat(a_b_s[:, :, 0], GROUP_DIM, axis=1)[:, None, :]
    w_single = jnp.repeat(a_w_s[:, :, 0, 0], GROUP_DIM, axis=1)[:, None, :]

    def mixer(x, l, *, tm, single):
        return _mixer_call(
            x, _row(norm_pre_mix[l]), (w_in, l), _row(a_ln_g[l]), _row(a_ln_b[l]),
            (w_single if single else a_w_s, l), (bias_single if single else bias_chunk, l),
            (w_out, l), _row(norm_post_mix[l]), tm=tm, single=single)

    def ffn_weights(l):
        return _row(norm_pre_ffn[l]), (w_up, l), (w_down, l), _row(norm_post_ffn[l])

    def tail(x, l, attn, *, tm):
        if attn is None:
            return _tail_call(x, None, None, None, *ffn_weights(l), tm=tm)
        return _tail_call(x, attn, (w_o, l - N_A_LAYERS), _row(norm_post_mix[l]),
                          *ffn_weights(l), tm=tm)

    def queries(x, l, tables, *, tm, seq_len, absorb):
        j = l - N_A_LAYERS
        return _q_call(x, _row(norm_pre_mix[l]), (w_dq, j), _row(b_q_norm[j]), (w_uq, j),
                       tables[2], tables[3], w_ukt, tm=tm, seq=seq_len, absorb=absorb)

    def latent(x, tables, *, tm, seq_len, expand):
        return _latent_call(x, _row(kv_norm_in), w_dkv_b, _row(kv_latent_norm), tables[0],
                            tables[1], w_uk2d, w_uv2d, tm=tm, seq=seq_len, expand=expand)

    assert N_B_LAYERS <= N_A_LAYERS
    ts = db * dseq
    tp = PROMPT_TOKEN_TILE
    tables_s = _rope_tables(jnp.full((ts,), n_past, dtype=jnp.int32), N_HEADS)
    tables_p = _rope_tables(jnp.arange(seq, dtype=jnp.int32), N_HEADS)

    xs = x_sample.reshape(ts, D_MODEL)
    v_rows = []
    for l in range(N_A_LAYERS):
        xs, v = mixer(xs, l, tm=ts, single=True)
        v_rows.append(v)
        xs = tail(xs, l, None, tm=ts)
    c_s, kr_s = latent(xs, tables_s, tm=ts, seq_len=ts, expand=False)
    q_s = queries(xs, N_A_LAYERS, tables_s, tm=ts, seq_len=ts, absorb=True)

    xp = x_prompt.reshape(nb * seq, D_MODEL)
    for l in range(N_A_LAYERS):
        (xp,) = mixer(xp, l, tm=tp, single=False)
        if l >= N_B_LAYERS:
            xp = tail(xp, l, None, tm=tp)
            continue
        lb = N_A_LAYERS + l
        xp, o_lat = _ffn_sample_attn_call(
            xp, *ffn_weights(l), page_table, jnp.transpose(q_s, (1, 0, 2)), c_s, kr_s,
            cache_kv_latent, cache_krt, ch=PAGES_PER_CHUNK, depth=PAGE_RING_DEPTH,
            rows=SAMPLE_ROWS_PER_STEP)
        attn_s = _uv_call(jnp.transpose(o_lat, (1, 0, 2)), w_uv_heads)
        xs = tail(xs, lb, attn_s, tm=ts)
        if l + 1 < N_B_LAYERS:
            q_s = queries(xs, lb + 1, tables_s, tm=ts, seq_len=ts, absorb=True)

    c_p, kr_p, kfull, vfull = latent(xp, tables_p, tm=tp, seq_len=seq, expand=True)
    for l in range(N_A_LAYERS, N_A_LAYERS + N_B_LAYERS):
        q_p = queries(xp, l, tables_p, tm=tp, seq_len=seq, absorb=False)
        attn_p = _prompt_attn_call(q_p, kfull, vfull, tq=ATTN_Q_TILE, tk=ATTN_KV_TILE,
                                   hb=ATTN_HEADS_PER_STEP)
        xp = tail(xp, l, attn_p.reshape(nb * seq, N_HEADS * V_HEAD), tm=tp)

    gate_v = jnp.stack(v_rows, axis=0).reshape(N_A_LAYERS, db, dseq, D_GATE)
    return (xp.reshape(nb, seq, D_MODEL), xs.reshape(db, dseq, D_MODEL),
            c_p.reshape(nb, seq, KV_LORA), kr_p.reshape(nb, seq, QK_ROPE),
            c_s.reshape(db, dseq, KV_LORA), kr_s.reshape(db, dseq, QK_ROPE), gate_v)


def kernel(x_prompt, x_sample, cache_kv_latent, cache_k_rope, page_table, norm_pre_mix, norm_post_mix, norm_pre_ffn, norm_post_ffn, a_w_in, a_ln_g, a_ln_b, a_w_s, a_b_s, a_w_out, kv_norm_in, w_dkv, kv_latent_norm, w_uk, w_uv, b_w_dq, b_q_norm, b_w_uq, b_w_o, ffn_w_up, ffn_w_down):
    assert x_sample.shape[1] == 1, "sample path assumes one new token per row"
    return _forward(x_prompt, x_sample, cache_kv_latent, cache_k_rope, page_table,
                    norm_pre_mix, norm_post_mix, norm_pre_ffn, norm_post_ffn,
                    a_w_in, a_ln_g, a_ln_b, a_w_s, a_b_s, a_w_out,
                    kv_norm_in, w_dkv, kv_latent_norm, w_uk, w_uv,
                    b_w_dq, b_q_norm, b_w_uq, b_w_o, ffn_w_up, ffn_w_down)
```

```python
import functools

import jax
import jax.numpy as jnp
from jax import lax
from jax.experimental import pallas as pl
from jax.experimental.pallas import tpu as pltpu

D_MODEL = 1024
CHUNK = 128
D_GATE = 2 * D_MODEL
N_GROUPS = 8
GROUP_DIM = D_GATE // N_GROUPS
D_FF = 4 * D_MODEL
N_HEADS = 16
QK_NOPE = 128
QK_ROPE = 64
QK_DIM = QK_NOPE + QK_ROPE
V_HEAD = 128
Q_LORA = 256
KV_LORA = 512
LAT_DIM = KV_LORA + QK_ROPE
ROPE_BASE = 10000.0
EPS = 1e-6
SM_SCALE = QK_DIM ** -0.5
N_A_LAYERS = 2
N_B_LAYERS = 2

F32 = jnp.float32
BF16 = jnp.bfloat16

VMEM_LIMIT_BYTES = 56 * 1024 * 1024

LANES = 128
V_EXT = 2 * LANES
LOG2E = 1.4426950408889634

PROMPT_TOKEN_TILE = 512
ATTN_Q_TILE = 512
ATTN_KV_TILE = 512
ATTN_HEADS_PER_STEP = 8
PAGES_PER_CHUNK = 16
PAGE_RING_DEPTH = 4
SAMPLE_ROWS_PER_STEP = 2


def _dot(a, b):
    return jnp.dot(a, b, preferred_element_type=F32)


def _dot_nt(a, b):
    return lax.dot_general(a, b, (((1,), (1,)), ((), ())), preferred_element_type=F32)


def _rms(x, g):
    return x * lax.rsqrt(jnp.mean(x * x, axis=-1, keepdims=True) + EPS) * g


def _gelu(z):
    return 0.5 * z * (1.0 + lax.erf(z * (0.5 ** 0.5)))


def _const_spec(shape):
    n = len(shape)
    return pl.BlockSpec(shape, lambda *_: (0,) * n, pipeline_mode=pl.Buffered(1))


def _resident(w):
    if isinstance(w, tuple):
        arr, layer = w
        n = arr.ndim - 1
        return arr, pl.BlockSpec((None,) + arr.shape[1:], lambda *_: (layer,) + (0,) * n,
                                 pipeline_mode=pl.Buffered(1))
    return w, _const_spec(w.shape)


def _params(n_grid):
    return pltpu.CompilerParams(
        dimension_semantics=("arbitrary",) * n_grid,
        vmem_limit_bytes=VMEM_LIMIT_BYTES,
    )


def _mixer_kernel(x_ref, gpre_ref, win_ref, lng_ref, lnb_ref, ws_ref, bias_ref, wout_ref,
                  gpost_ref, *rest, tm, single):
    if single:
        o_ref, v_ref, gated_ref = rest
    else:
        o_ref, gated_ref = rest
    x = x_ref[...]
    h = _rms(x, gpre_ref[...]).astype(BF16)
    v = _gelu(_dot(h, win_ref[:, D_GATE:]))
    u = _gelu(_dot(h, win_ref[:, :D_GATE]))
    mu = jnp.mean(v, axis=-1, keepdims=True)
    vc = v - mu
    v = vc * lax.rsqrt(jnp.mean(vc * vc, axis=-1, keepdims=True) + EPS) * lng_ref[...] + lnb_ref[...]
    if single:
        v_ref[...] = v
        gated_ref[...] = (u * (v * ws_ref[...] + bias_ref[...])).astype(BF16)
    else:
        vb = v.astype(BF16)
        row = lax.broadcasted_iota(jnp.int32, (CHUNK, CHUNK), 0)
        col = lax.broadcasted_iota(jnp.int32, (CHUNK, CHUNK), 1)
        causal = row >= col
        for g in range(N_GROUPS):
            wg = jnp.where(causal, ws_ref[g], 0.0).astype(BF16)
            cols = slice(g * GROUP_DIM, (g + 1) * GROUP_DIM)
            for c in range(tm // CHUNK):
                rows = slice(c * CHUNK, (c + 1) * CHUNK)
                s = _dot(wg, vb[rows, cols]) + bias_ref[:, cols]
                gated_ref[rows, cols] = (u[rows, cols] * s).astype(BF16)
    m = _dot(gated_ref[...], wout_ref[...])
    o_ref[...] = x + _rms(m, gpost_ref[...])


def _mixer_call(x, gpre, win, lng, lnb, ws, bias, wout, gpost, *, tm, single):
    t = x.shape[0]
    tok = lambda w: pl.BlockSpec((tm, w), lambda i: (i, 0))
    out_shape = [jax.ShapeDtypeStruct((t, D_MODEL), F32)]
    out_specs = [tok(D_MODEL)]
    if single:
        out_shape.append(jax.ShapeDtypeStruct((t, D_GATE), F32))
        out_specs.append(tok(D_GATE))
    weights, weight_specs = zip(*map(_resident, (gpre, win, lng, lnb, ws, bias, wout, gpost)))
    return pl.pallas_call(
        functools.partial(_mixer_kernel, tm=tm, single=single),
        grid=(t // tm,),
        in_specs=[tok(D_MODEL), *weight_specs],
        out_specs=out_specs,
        out_shape=out_shape,
        scratch_shapes=[pltpu.VMEM((tm, D_GATE), BF16)],
        compiler_params=_params(1),
        name="mixer_single" if single else "mixer_chunk",
    )(x, *weights)


def _tail_kernel(*refs, has_oproj):
    if has_oproj:
        x_ref, a_ref, wo_ref, gpm_ref, gpf_ref, wup_ref, wdn_ref, gpo_ref, o_ref = refs
    else:
        x_ref, gpf_ref, wup_ref, wdn_ref, gpo_ref, o_ref = refs
    x = x_ref[...]
    if has_oproj:
        x = x + _rms(_dot(a_ref[...], wo_ref[...]), gpm_ref[...])
    h = _rms(x, gpf_ref[...]).astype(BF16)
    a = jnp.square(jnp.maximum(_dot(h, wup_ref[...]), 0.0)).astype(BF16)
    o_ref[...] = x + _rms(_dot(a, wdn_ref[...]), gpo_ref[...])


def _tail_call(x, attn, wo, gpm, gpf, wup, wdn, gpo, *, tm):
    t = x.shape[0]
    tok = lambda w: pl.BlockSpec((tm, w), lambda i: (i, 0))
    has_oproj = attn is not None
    if has_oproj:
        weights, weight_specs = zip(*map(_resident, (wo, gpm, gpf, wup, wdn, gpo)))
        args = (x, attn, *weights)
        in_specs = [tok(D_MODEL), tok(attn.shape[1]), *weight_specs]
    else:
        weights, weight_specs = zip(*map(_resident, (gpf, wup, wdn, gpo)))
        args = (x, *weights)
        in_specs = [tok(D_MODEL), *weight_specs]
    return pl.pallas_call(
        functools.partial(_tail_kernel, has_oproj=has_oproj),
        grid=(t // tm,),
        in_specs=in_specs,
        out_specs=tok(D_MODEL),
        out_shape=jax.ShapeDtypeStruct((t, D_MODEL), F32),
        compiler_params=_params(1),
        name="tail_oproj" if has_oproj else "tail_ffn",
    )(*args)


def _latent_kernel(*refs, expand):
    if expand:
        (x_ref, gin_ref, wdkv_ref, glat_ref, cos_ref, sin_ref, wuk_ref, wuv_ref,
         c_ref, kr_ref, kfull_ref, v_ref) = refs
    else:
        x_ref, gin_ref, wdkv_ref, glat_ref, cos_ref, sin_ref, c_ref, kr_ref = refs
    h = _rms(x_ref[...], gin_ref[...]).astype(BF16)
    ckr = _dot(h, wdkv_ref[...])
    c = _rms(ckr[:, :KV_LORA], glat_ref[...])
    kr = ckr[:, KV_LORA:LAT_DIM] * cos_ref[...] + ckr[:, LAT_DIM:] * sin_ref[...]
    c_ref[...] = c
    kr_ref[...] = kr
    if expand:
        cb = c.astype(BF16)
        krb = kr.astype(BF16)
        kn = _dot(cb, wuk_ref[...])
        vv = _dot(cb, wuv_ref[...])
        lane = lax.broadcasted_iota(jnp.int32, (cb.shape[0], V_EXT - V_HEAD), 1)
        ones_col = jnp.where(lane == 0, 1.0, 0.0).astype(BF16)
        for hd in range(N_HEADS):
            kfull_ref[0, hd, :, 0:QK_NOPE] = kn[:, hd * QK_NOPE:(hd + 1) * QK_NOPE].astype(BF16)
            kfull_ref[0, hd, :, QK_NOPE:QK_DIM] = krb
            v_ref[0, hd, :, 0:V_HEAD] = vv[:, hd * V_HEAD:(hd + 1) * V_HEAD].astype(BF16)
            v_ref[0, hd, :, V_HEAD:V_EXT] = ones_col


def _latent_call(x, gin, wdkv, glat, cos, sin, wuk, wuv, *, tm, seq, expand):
    t = x.shape[0]
    per_seq = seq // tm
    tok = lambda w: pl.BlockSpec((tm, w), lambda i: (i, 0))
    pos = lambda w: pl.BlockSpec((tm, w), lambda i: (i % per_seq, 0))
    args = [x, gin, wdkv, glat, cos, sin]
    in_specs = [tok(D_MODEL), _const_spec(gin.shape), _const_spec(wdkv.shape),
                _const_spec(glat.shape), pos(QK_ROPE), pos(QK_ROPE)]
    out_shape = [jax.ShapeDtypeStruct((t, KV_LORA), F32), jax.ShapeDtypeStruct((t, QK_ROPE), F32)]
    out_specs = [tok(KV_LORA), tok(QK_ROPE)]
    if expand:
        args += [wuk, wuv]
        in_specs += [_const_spec(wuk.shape), _const_spec(wuv.shape)]
        nb = t // seq
        head_major = lambda w: pl.BlockSpec(
            (1, N_HEADS, tm, w), lambda i: (i // per_seq, 0, i % per_seq, 0))
        out_shape += [jax.ShapeDtypeStruct((nb, N_HEADS, seq, QK_DIM), BF16),
                      jax.ShapeDtypeStruct((nb, N_HEADS, seq, V_EXT), BF16)]
        out_specs += [head_major(QK_DIM), head_major(V_EXT)]
    return pl.pallas_call(
        functools.partial(_latent_kernel, expand=expand),
        grid=(t // tm,),
        in_specs=in_specs,
        out_specs=out_specs,
        out_shape=out_shape,
        compiler_params=_params(1),
        name="latent_expand" if expand else "latent",
    )(*args)


def _q_kernel(*refs, absorb):
    if absorb:
        x_ref, gpre_ref, wdq_ref, gq_ref, wuq_ref, cos_ref, sin_ref, wukt_ref, q_ref = refs
    else:
        x_ref, gpre_ref, wdq_ref, gq_ref, wuq_ref, cos_ref, sin_ref, q_ref = refs
    h = _rms(x_ref[...], gpre_ref[...]).astype(BF16)
    cq = _rms(_dot(h, wdq_ref[...]), gq_ref[...]).astype(BF16)
    q = _dot(cq, wuq_ref[...])
    n_nope = N_HEADS * QK_NOPE
    n_rope = N_HEADS * QK_ROPE
    qn = q[:, :n_nope]
    qr = q[:, n_nope:n_nope + n_rope] * cos_ref[...] + q[:, n_nope + n_rope:] * sin_ref[...]
    scale = SM_SCALE if absorb else SM_SCALE * LOG2E
    for hd in range(N_HEADS):
        qn_h = qn[:, hd * QK_NOPE:(hd + 1) * QK_NOPE]
        qr_h = (qr[:, hd * QK_ROPE:(hd + 1) * QK_ROPE] * scale).astype(BF16)
        if absorb:
            q_lat = _dot(qn_h.astype(BF16), wukt_ref[hd])
            q_ref[hd, :, 0:KV_LORA] = (q_lat * scale).astype(BF16)
            q_ref[hd, :, KV_LORA:LAT_DIM] = qr_h
        else:
            q_ref[0, hd, :, 0:QK_NOPE] = (qn_h * scale).astype(BF16)
            q_ref[0, hd, :, QK_NOPE:QK_DIM] = qr_h


def _q_call(x, gpre, wdq, gq, wuq, cos, sin, wukt, *, tm, seq, absorb):
    t = x.shape[0]
    per_seq = seq // tm
    width = N_HEADS * QK_ROPE
    tok = lambda w: pl.BlockSpec((tm, w), lambda i: (i, 0))
    pos = lambda w: pl.BlockSpec((tm, w), lambda i: (i % per_seq, 0))
    weights, weight_specs = zip(*map(_resident, (gpre, wdq, gq, wuq)))
    args = [x, *weights, cos, sin]
    in_specs = [tok(D_MODEL), *weight_specs, pos(width), pos(width)]
    if absorb:
        args.append(wukt)
        in_specs.append(_const_spec(wukt.shape))
        out_shape = jax.ShapeDtypeStruct((N_HEADS, t, LAT_DIM), BF16)
        out_spec = pl.BlockSpec((N_HEADS, tm, LAT_DIM), lambda i: (0, i, 0))
    else:
        out_shape = jax.ShapeDtypeStruct((t // seq, N_HEADS, seq, QK_DIM), BF16)
        out_spec = pl.BlockSpec((1, N_HEADS, tm, QK_DIM),
                                lambda i: (i // per_seq, 0, i % per_seq, 0))
    return pl.pallas_call(
        functools.partial(_q_kernel, absorb=absorb),
        grid=(t // tm,),
        in_specs=in_specs,
        out_specs=out_spec,
        out_shape=out_shape,
        compiler_params=_params(1),
        name="q_absorb" if absorb else "q_heads",
    )(*args)


def _prompt_attn_kernel(q_ref, k_ref, v_ref, o_ref, m_ref, acc_ref, *, tq, tk, hb):
    i = pl.program_id(2)
    n_full = (i * tq) // tk
    m_ref[...] = jnp.full(m_ref.shape, -jnp.inf, F32)
    acc_ref[...] = jnp.zeros(acc_ref.shape, F32)

    def step(start, mask):
        def scores(hh):
            return _dot_nt(q_ref[0, hh], k_ref[0, hh, pl.ds(start, tk), :])

        s_next = scores(0)
        for hh in range(hb):
            s = s_next
            if hh + 1 < hb:
                s_next = scores(hh + 1)
            v = v_ref[0, hh, pl.ds(start, tk), :]
            if mask is not None:
                s = jnp.where(mask, s, -jnp.inf)
            m_prev = m_ref[hh]
            m_next = jnp.maximum(m_prev, jnp.max(s, axis=-1, keepdims=True))
            alpha = jnp.exp2(m_prev - m_next)
            p = jnp.exp2(s - jnp.concatenate([m_next] * (tk // LANES), axis=1))
            acc_ref[hh] = (acc_ref[hh] * jnp.concatenate([alpha] * (V_EXT // LANES), axis=1)
                           + _dot(p.astype(BF16), v))
            m_ref[hh] = m_next

    def full_step(j, carry):
        step(pl.multiple_of(j * tk, tk), None)
        return carry

    lax.fori_loop(0, n_full, full_step, 0)
    row = lax.broadcasted_iota(jnp.int32, (tq, tk), 0)
    col = lax.broadcasted_iota(jnp.int32, (tq, tk), 1)
    for d in range(tq // tk):
        step(pl.multiple_of(i * tq + d * tk, tk), col + d * tk <= row)
    for hh in range(hb):
        acc = acc_ref[hh]
        o = acc[:, :V_HEAD] / acc[:, V_HEAD:V_HEAD + 1]
        o_ref[0, :, hh * V_HEAD:(hh + 1) * V_HEAD] = o.astype(o_ref.dtype)


def _prompt_attn_call(q, k, v, *, tq, tk, hb):
    nb, nh, seq, _ = q.shape
    assert tq % tk == 0 and seq % tq == 0 and nh % hb == 0
    return pl.pallas_call(
        functools.partial(_prompt_attn_kernel, tq=tq, tk=tk, hb=hb),
        grid=(nb, nh // hb, seq // tq),
        in_specs=[pl.BlockSpec((1, hb, tq, QK_DIM), lambda b, h, i: (b, h, i, 0)),
                  pl.BlockSpec((1, hb, seq, QK_DIM), lambda b, h, i: (b, h, 0, 0)),
                  pl.BlockSpec((1, hb, seq, V_EXT), lambda b, h, i: (b, h, 0, 0))],
        out_specs=pl.BlockSpec((1, tq, hb * V_HEAD), lambda b, h, i: (b, i, h)),
        out_shape=jax.ShapeDtypeStruct((nb, seq, nh * V_HEAD), BF16),
        scratch_shapes=[pltpu.VMEM((hb, tq, LANES), F32),
                        pltpu.VMEM((hb, tq, V_EXT), F32)],
        compiler_params=_params(3),
        name="prompt_attn",
    )(q, k, v)


def _sample_attn_kernel(pt_ref, q_ref, cnew_ref, krnew_ref, cache_c_ref, cache_krt_ref, o_ref,
                        cbuf_ref, krbuf_ref, sem_ref, kc_ref, krt_ref,
                        *, n_pages, ch, depth, rows, between_chunks=None):
    n_rows = pl.num_programs(0) * rows
    n_chunks = n_pages // ch
    page = cbuf_ref.shape[1] // ch

    def chunk_copies(row, chunk):
        slot = chunk % depth
        copies = []
        for r in range(ch):
            pg = pt_ref[row * n_pages + chunk * ch + r]
            copies.append(pltpu.make_async_copy(
                cache_c_ref.at[pg], cbuf_ref.at[slot, pl.ds(r * page, page), :],
                sem_ref.at[slot, r]))
            copies.append(pltpu.make_async_copy(
                cache_krt_ref.at[pg], krbuf_ref.at[slot, r], sem_ref.at[slot, ch + r]))
        return copies

    @pl.when(pl.program_id(0) == 0)
    def _():
        for chunk in range(depth - 1):
            for cp in chunk_copies(0, chunk):
                cp.start()

    def attend(r_local, q, c_new, kr_new):
        b = pl.program_id(0) * rows + r_local
        q_lat = q[:, :KV_LORA]
        q_rope = q[:, KV_LORA:]

        def scores(chunk):
            ahead = chunk + depth - 1
            if ahead < n_chunks:
                for cp in chunk_copies(b, ahead):
                    cp.start()
            else:
                @pl.when(b + 1 < n_rows)
                def _():
                    for cp in chunk_copies(b + 1, ahead - n_chunks):
                        cp.start()
            for cp in chunk_copies(b, chunk):
                cp.wait()
            slot = chunk % depth
            half = chunk % 2
            kc_ref[half] = cbuf_ref[slot].astype(BF16)
            for r in range(ch):
                krt_ref[half, :, r * page:(r + 1) * page] = krbuf_ref[slot, r].astype(BF16)
            return _dot_nt(q_lat, kc_ref[half]) + _dot(q_rope, krt_ref[half])

        m = (jnp.sum(q_lat.astype(F32) * c_new, axis=-1, keepdims=True)
             + jnp.sum(q_rope.astype(F32) * kr_new, axis=-1, keepdims=True))
        l = jnp.ones_like(m)
        acc = jnp.broadcast_to(c_new, (N_HEADS, KV_LORA))

        s_next = scores(0)
        for chunk in range(n_chunks):
            s = s_next
            if chunk + 1 < n_chunks:
                s_next = scores(chunk + 1)
            if between_chunks is not None:
                between_chunks(r_local * n_chunks + chunk)
            m_new = jnp.maximum(m, jnp.max(s, axis=-1, keepdims=True))
            alpha = jnp.exp(m - m_new)
            p = jnp.exp(s - m_new)
            l = alpha * l + jnp.sum(p, axis=-1, keepdims=True)
            acc = alpha * acc + _dot(p.astype(BF16), kc_ref[chunk % 2])
            m = m_new
        return acc / l

    for r in range(rows):
        o_ref[r] = attend(r, q_ref[r], cnew_ref[r], krnew_ref[r]).astype(o_ref.dtype)


N_TAIL_FFN_INPUTS = 5
N_SAMPLE_ATTN_INPUTS = 5


def _ffn_sample_attn_kernel(pt_ref, *refs, n_pages, ch, depth, rows):
    x_ref, gpf_ref, wup_ref, wdn_ref, gpo_ref = refs[:N_TAIL_FFN_INPUTS]
    sample_in = refs[N_TAIL_FFN_INPUTS:N_TAIL_FFN_INPUTS + N_SAMPLE_ATTN_INPUTS]
    x_out_ref, o_lat_ref = refs[N_TAIL_FFN_INPUTS + N_SAMPLE_ATTN_INPUTS:][:2]
    scratch = refs[N_TAIL_FFN_INPUTS + N_SAMPLE_ATTN_INPUTS + 2:]
    n_slices = rows * (n_pages // ch)
    ff = wup_ref.shape[1] // n_slices
    x = x_ref[...]
    h = _rms(x, gpf_ref[...]).astype(BF16)
    mlp = [jnp.zeros(x.shape, F32)]

    def mlp_slice(k):
        cols = slice(k * ff, (k + 1) * ff)
        a = jnp.square(jnp.maximum(_dot(h, wup_ref[:, cols]), 0.0)).astype(BF16)
        mlp[0] = mlp[0] + _dot(a, wdn_ref[cols, :])

    _sample_attn_kernel(pt_ref, *sample_in, o_lat_ref, *scratch, n_pages=n_pages, ch=ch,
                        depth=depth, rows=rows, between_chunks=mlp_slice)
    x_out_ref[...] = x + _rms(mlp[0], gpo_ref[...])


def _ffn_sample_attn_call(x, gpf, wup, wdn, gpo, page_table, q, c_new, kr_new, cache_c,
                          cache_krt, *, ch, depth, rows):
    t = x.shape[0]
    nb, n_pages = page_table.shape
    page = cache_c.shape[1]
    n_chunks = n_pages // ch
    n_steps = nb // rows
    tm = t // n_steps
    assert nb % rows == 0 and t % n_steps == 0 and tm % 8 == 0
    assert n_pages % ch == 0 and n_chunks % depth == 0 and depth >= 2
    tok = lambda w: pl.BlockSpec((tm, w), lambda b, pt: (b, 0))
    row = lambda h, w: pl.BlockSpec((rows, h, w), lambda b, pt: (b, 0, 0))
    weights, weight_specs = zip(*map(_resident, (gpf, wup, wdn, gpo)))
    return pl.pallas_call(
        functools.partial(_ffn_sample_attn_kernel, n_pages=n_pages, ch=ch, depth=depth,
                          rows=rows),
        grid_spec=pltpu.PrefetchScalarGridSpec(
            num_scalar_prefetch=1,
            grid=(n_steps,),
            in_specs=[tok(D_MODEL), *weight_specs,
                      row(N_HEADS, LAT_DIM), row(1, KV_LORA), row(1, QK_ROPE),
                      pl.BlockSpec(memory_space=pl.ANY), pl.BlockSpec(memory_space=pl.ANY)],
            out_specs=[tok(D_MODEL), row(N_HEADS, KV_LORA)],
            scratch_shapes=[pltpu.VMEM((depth, ch * page, KV_LORA), F32),
                            pltpu.VMEM((depth, ch, QK_ROPE, page), F32),
                            pltpu.SemaphoreType.DMA((depth, 2 * ch)),
                            pltpu.VMEM((2, ch * page, KV_LORA), BF16),
                            pltpu.VMEM((2, QK_ROPE, ch * page), BF16)],
        ),
        out_shape=[jax.ShapeDtypeStruct((t, D_MODEL), F32),
                   jax.ShapeDtypeStruct((nb, N_HEADS, KV_LORA), BF16)],
        compiler_params=_params(1),
        name="ffn_sample_attn",
    )(page_table.reshape(-1), x, *weights, q, c_new.reshape(nb, 1, KV_LORA),
      kr_new.reshape(nb, 1, QK_ROPE), cache_c, cache_krt)


def _uv_kernel(o_ref, wuv_ref, out_ref):
    for hd in range(N_HEADS):
        out_ref[:, hd * V_HEAD:(hd + 1) * V_HEAD] = _dot(o_ref[hd], wuv_ref[hd]).astype(out_ref.dtype)


def _uv_call(o_lat, wuv_heads):
    nh, t, _ = o_lat.shape
    return pl.pallas_call(
        _uv_kernel,
        out_shape=jax.ShapeDtypeStruct((t, nh * V_HEAD), BF16),
        compiler_params=pltpu.CompilerParams(vmem_limit_bytes=VMEM_LIMIT_BYTES),
        name="uv_proj",
    )(o_lat, wuv_heads)


def _rope_tables(pos, reps):
    half = QK_ROPE // 2
    inv = ROPE_BASE ** (-jnp.arange(half, dtype=F32) / half)
    ang = pos.astype(F32)[:, None] * inv[None, :]
    cos = jnp.cos(ang)
    sin = jnp.sin(ang)
    cos64 = jnp.concatenate([cos, cos], axis=1)
    sin64 = jnp.concatenate([-sin, sin], axis=1)
    return cos64, sin64, jnp.tile(cos64, (1, reps)), jnp.tile(sin64, (1, reps))


def _row(v):
    return v.reshape(1, -1).astype(F32)


@jax.jit
def _forward(x_prompt, x_sample, cache_kv_latent, cache_k_rope, page_table,
             norm_pre_mix, norm_post_mix, norm_pre_ffn, norm_post_ffn,
             a_w_in, a_ln_g, a_ln_b, a_w_s, a_b_s, a_w_out,
             kv_norm_in, w_dkv, kv_latent_norm, w_uk, w_uv,
             b_w_dq, b_q_norm, b_w_uq, b_w_o, ffn_w_up, ffn_w_down):
    nb, seq, _ = x_prompt.shape
    db, dseq, _ = x_sample.shape
    n_pages = page_table.shape[1]
    n_past = n_pages * cache_kv_latent.shape[1]

    bf = lambda w: w.astype(BF16)
    w_in = bf(a_w_in)
    w_out = bf(a_w_out)
    w_up = bf(ffn_w_up)
    w_down = bf(ffn_w_down)
    half = QK_ROPE // 2
    swap_halves = lambda w: jnp.concatenate([w[..., half:], w[..., :half]], axis=-1)
    w_dkv_b = bf(jnp.concatenate([w_dkv, swap_halves(w_dkv[:, KV_LORA:])], axis=1))
    w_uk2d = bf(w_uk.reshape(KV_LORA, N_HEADS * QK_NOPE))
    w_uv2d = bf(w_uv.reshape(KV_LORA, N_HEADS * V_HEAD))
    w_ukt = bf(jnp.transpose(w_uk, (1, 2, 0)))
    w_uv_heads = bf(jnp.transpose(w_uv, (1, 0, 2)))
    w_dq = bf(b_w_dq)
    w_o = bf(b_w_o)
    cache_krt = jnp.transpose(cache_k_rope, (0, 2, 1))
    w_uq4 = b_w_uq.reshape(N_B_LAYERS, Q_LORA, N_HEADS, QK_DIM)
    w_uq_rope = w_uq4[..., QK_NOPE:]
    w_uq = bf(jnp.concatenate(
        [w_uq4[..., :QK_NOPE].reshape(N_B_LAYERS, Q_LORA, N_HEADS * QK_NOPE),
         w_uq_rope.reshape(N_B_LAYERS, Q_LORA, N_HEADS * QK_ROPE),
         swap_halves(w_uq_rope).reshape(N_B_LAYERS, Q_LORA, N_HEADS * QK_ROPE)], axis=-1))

    bias_chunk = jnp.repeat(jnp.transpose(a_b_s, (0, 2, 1)), GROUP_DIM, axis=2)
    bias_single = jnp.repeat(a_b_s[:, :, 0], GROUP_DIM, axis=1)[:, None, :]
    w_single = jnp.repeat(a_w_s[:, :, 0, 0], GROUP_DIM, axis=1)[:, None, :]

    def mixer(x, l, *, tm, single):
        return _mixer_call(
            x, _row(norm_pre_mix[l]), (w_in, l), _row(a_ln_g[l]), _row(a_ln_b[l]),
            (w_single if single else a_w_s, l), (bias_single if single else bias_chunk, l),
            (w_out, l), _row(norm_post_mix[l]), tm=tm, single=single)

    def ffn_weights(l):
        return _row(norm_pre_ffn[l]), (w_up, l), (w_down, l), _row(norm_post_ffn[l])

    def tail(x, l, attn, *, tm):
        if attn is None:
            return _tail_call(x, None, None, None, *ffn_weights(l), tm=tm)
        return _tail_call(x, attn, (w_o, l - N_A_LAYERS), _row(norm_post_mix[l]),
                          *ffn_weights(l), tm=tm)

    def queries(x, l, tables, *, tm, seq_len, absorb):
        j = l - N_A_LAYERS
        return _q_call(x, _row(norm_pre_mix[l]), (w_dq, j), _row(b_q_norm[j]), (w_uq, j),
                       tables[2], tables[3], w_ukt, tm=tm, seq=seq_len, absorb=absorb)

    def latent(x, tables, *, tm, seq_len, expand):
        return _latent_call(x, _row(kv_norm_in), w_dkv_b, _row(kv_latent_norm), tables[0],
                            tables[1], w_uk2d, w_uv2d, tm=tm, seq=seq_len, expand=expand)

    assert N_B_LAYERS <= N_A_LAYERS
    ts = db * dseq
    tp = PROMPT_TOKEN_TILE
    tables_s = _rope_tables(jnp.full((ts,), n_past, dtype=jnp.int32), N_HEADS)
    tables_p = _rope_tables(jnp.arange(seq, dtype=jnp.int32), N_HEADS)

    xs = x_sample.reshape(ts, D_MODEL)
    v_rows = []
    for l in range(N_A_LAYERS):
        xs, v = mixer(xs, l, tm=ts, single=True)
        v_rows.append(v)
        xs = tail(xs, l, None, tm=ts)
    c_s, kr_s = latent(xs, tables_s, tm=ts, seq_len=ts, expand=False)
    q_s = queries(xs, N_A_LAYERS, tables_s, tm=ts, seq_len=ts, absorb=True)

    xp = x_prompt.reshape(nb * seq, D_MODEL)
    for l in range(N_A_LAYERS):
        (xp,) = mixer(xp, l, tm=tp, single=False)
        if l >= N_B_LAYERS:
            xp = tail(xp, l, None, tm=tp)
            continue
        lb = N_A_LAYERS + l
        xp, o_lat = _ffn_sample_attn_call(
            xp, *ffn_weights(l), page_table, jnp.transpose(q_s, (1, 0, 2)), c_s, kr_s,
            cache_kv_latent, cache_krt, ch=PAGES_PER_CHUNK, depth=PAGE_RING_DEPTH,
            rows=SAMPLE_ROWS_PER_STEP)
        attn_s = _uv_call(jnp.transpose(o_lat, (1, 0, 2)), w_uv_heads)
        xs = tail(xs, lb, attn_s, tm=ts)
        if l + 1 < N_B_LAYERS:
            q_s = queries(xs, lb + 1, tables_s, tm=ts, seq_len=ts, absorb=True)

    c_p, kr_p, kfull, vfull = latent(xp, tables_p, tm=tp, seq_len=seq, expand=True)
    for l in range(N_A_LAYERS, N_A_LAYERS + N_B_LAYERS):
        q_p = queries(xp, l, tables_p, tm=tp, seq_len=seq, absorb=False)
        attn_p = _prompt_attn_call(q_p, kfull, vfull, tq=ATTN_Q_TILE, tk=ATTN_KV_TILE,
                                   hb=ATTN_HEADS_PER_STEP)
        xp = tail(xp, l, attn_p.reshape(nb * seq, N_HEADS * V_HEAD), tm=tp)

    gate_v = jnp.stack(v_rows, axis=0).reshape(N_A_LAYERS, db, dseq, D_GATE)
    return (xp.reshape(nb, seq, D_MODEL), xs.reshape(db, dseq, D_MODEL),
            c_p.reshape(nb, seq, KV_LORA), kr_p.reshape(nb, seq, QK_ROPE),
            c_s.reshape(db, dseq, KV_LORA), kr_s.reshape(db, dseq, QK_ROPE), gate_v)


def kernel(x_prompt, x_sample, cache_kv_latent, cache_k_rope, page_table, norm_pre_mix, norm_post_mix, norm_pre_ffn, norm_post_ffn, a_w_in, a_ln_g, a_ln_b, a_w_s, a_b_s, a_w_out, kv_norm_in, w_dkv, kv_latent_norm, w_uk, w_uv, b_w_dq, b_q_norm, b_w_uq, b_w_o, ffn_w_up, ffn_w_down):
    assert x_sample.shape[1] == 1, "sample path assumes one new token per row"
    return _forward(x_prompt, x_sample, cache_kv_latent, cache_k_rope, page_table,
                    norm_pre_mix, norm_post_mix, norm_pre_ffn, norm_post_ffn,
                    a_w_in, a_ln_g, a_ln_b, a_w_s, a_b_s, a_w_out,
                    kv_norm_in, w_dkv, kv_latent_norm, w_uk, w_uv,
                    b_w_dq, b_q_norm, b_w_uq, b_w_o, ffn_w_up, ffn_w_down)
```

```python
import functools

import jax
import jax.numpy as jnp
from jax import lax
from jax.experimental import pallas as pl
from jax.experimental.pallas import tpu as pltpu

D_MODEL = 1024
CHUNK = 128
D_GATE = 2 * D_MODEL
N_GROUPS = 8
GROUP_DIM = D_GATE // N_GROUPS
D_FF = 4 * D_MODEL
N_HEADS = 16
QK_NOPE = 128
QK_ROPE = 64
QK_DIM = QK_NOPE + QK_ROPE
V_HEAD = 128
Q_LORA = 256
KV_LORA = 512
LAT_DIM = KV_LORA + QK_ROPE
ROPE_BASE = 10000.0
EPS = 1e-6
SM_SCALE = QK_DIM ** -0.5
N_A_LAYERS = 2
N_B_LAYERS = 2

F32 = jnp.float32
BF16 = jnp.bfloat16

VMEM_LIMIT_BYTES = 56 * 1024 * 1024

LANES = 128
V_EXT = 2 * LANES
LOG2E = 1.4426950408889634

PROMPT_TOKEN_TILE = 512
ATTN_Q_TILE = 512
ATTN_KV_TILE = 512
ATTN_HEADS_PER_STEP = 8
PAGES_PER_CHUNK = 16
PAGE_RING_DEPTH = 4
SAMPLE_ROWS_PER_STEP = 1


def _dot(a, b):
    return jnp.dot(a, b, preferred_element_type=F32)


def _dot_nt(a, b):
    return lax.dot_general(a, b, (((1,), (1,)), ((), ())), preferred_element_type=F32)


def _rms(x, g):
    return x * lax.rsqrt(jnp.mean(x * x, axis=-1, keepdims=True) + EPS) * g


def _gelu(z):
    return 0.5 * z * (1.0 + lax.erf(z * (0.5 ** 0.5)))


def _const_spec(shape):
    n = len(shape)
    return pl.BlockSpec(shape, lambda *_: (0,) * n, pipeline_mode=pl.Buffered(1))


def _resident(w):
    if isinstance(w, tuple):
        arr, layer = w
        n = arr.ndim - 1
        return arr, pl.BlockSpec((None,) + arr.shape[1:], lambda *_: (layer,) + (0,) * n,
                                 pipeline_mode=pl.Buffered(1))
    return w, _const_spec(w.shape)


def _params(n_grid):
    return pltpu.CompilerParams(
        dimension_semantics=("arbitrary",) * n_grid,
        vmem_limit_bytes=VMEM_LIMIT_BYTES,
    )


def _mixer_kernel(x_ref, gpre_ref, win_ref, lng_ref, lnb_ref, ws_ref, bias_ref, wout_ref,
                  gpost_ref, *rest, tm, single):
    if single:
        o_ref, v_ref, gated_ref = rest
    else:
        o_ref, gated_ref = rest
    x = x_ref[...]
    h = _rms(x, gpre_ref[...]).astype(BF16)
    v = _gelu(_dot(h, win_ref[:, D_GATE:]))
    u = _gelu(_dot(h, win_ref[:, :D_GATE]))
    mu = jnp.mean(v, axis=-1, keepdims=True)
    vc = v - mu
    v = vc * lax.rsqrt(jnp.mean(vc * vc, axis=-1, keepdims=True) + EPS) * lng_ref[...] + lnb_ref[...]
    if single:
        v_ref[...] = v
        gated_ref[...] = (u * (v * ws_ref[...] + bias_ref[...])).astype(BF16)
    else:
        vb = v.astype(BF16)
        row = lax.broadcasted_iota(jnp.int32, (CHUNK, CHUNK), 0)
        col = lax.broadcasted_iota(jnp.int32, (CHUNK, CHUNK), 1)
        causal = row >= col
        for g in range(N_GROUPS):
            wg = jnp.where(causal, ws_ref[g], 0.0).astype(BF16)
            cols = slice(g * GROUP_DIM, (g + 1) * GROUP_DIM)
            for c in range(tm // CHUNK):
                rows = slice(c * CHUNK, (c + 1) * CHUNK)
                s = _dot(wg, vb[rows, cols]) + bias_ref[:, cols]
                gated_ref[rows, cols] = (u[rows, cols] * s).astype(BF16)
    m = _dot(gated_ref[...], wout_ref[...])
    o_ref[...] = x + _rms(m, gpost_ref[...])


def _mixer_call(x, gpre, win, lng, lnb, ws, bias, wout, gpost, *, tm, single):
    t = x.shape[0]
    tok = lambda w: pl.BlockSpec((tm, w), lambda i: (i, 0))
    out_shape = [jax.ShapeDtypeStruct((t, D_MODEL), F32)]
    out_specs = [tok(D_MODEL)]
    if single:
        out_shape.append(jax.ShapeDtypeStruct((t, D_GATE), F32))
        out_specs.append(tok(D_GATE))
    weights, weight_specs = zip(*map(_resident, (gpre, win, lng, lnb, ws, bias, wout, gpost)))
    return pl.pallas_call(
        functools.partial(_mixer_kernel, tm=tm, single=single),
        grid=(t // tm,),
        in_specs=[tok(D_MODEL), *weight_specs],
        out_specs=out_specs,
        out_shape=out_shape,
        scratch_shapes=[pltpu.VMEM((tm, D_GATE), BF16)],
        compiler_params=_params(1),
        name="mixer_single" if single else "mixer_chunk",
    )(x, *weights)


def _tail_kernel(*refs, has_oproj):
    if has_oproj:
        x_ref, a_ref, wo_ref, gpm_ref, gpf_ref, wup_ref, wdn_ref, gpo_ref, o_ref = refs
    else:
        x_ref, gpf_ref, wup_ref, wdn_ref, gpo_ref, o_ref = refs
    x = x_ref[...]
    if has_oproj:
        x = x + _rms(_dot(a_ref[...], wo_ref[...]), gpm_ref[...])
    h = _rms(x, gpf_ref[...]).astype(BF16)
    a = jnp.square(jnp.maximum(_dot(h, wup_ref[...]), 0.0)).astype(BF16)
    o_ref[...] = x + _rms(_dot(a, wdn_ref[...]), gpo_ref[...])


def _tail_call(x, attn, wo, gpm, gpf, wup, wdn, gpo, *, tm):
    t = x.shape[0]
    tok = lambda w: pl.BlockSpec((tm, w), lambda i: (i, 0))
    has_oproj = attn is not None
    if has_oproj:
        weights, weight_specs = zip(*map(_resident, (wo, gpm, gpf, wup, wdn, gpo)))
        args = (x, attn, *weights)
        in_specs = [tok(D_MODEL), tok(attn.shape[1]), *weight_specs]
    else:
        weights, weight_specs = zip(*map(_resident, (gpf, wup, wdn, gpo)))
        args = (x, *weights)
        in_specs = [tok(D_MODEL), *weight_specs]
    return pl.pallas_call(
        functools.partial(_tail_kernel, has_oproj=has_oproj),
        grid=(t // tm,),
        in_specs=in_specs,
        out_specs=tok(D_MODEL),
        out_shape=jax.ShapeDtypeStruct((t, D_MODEL), F32),
        compiler_params=_params(1),
        name="tail_oproj" if has_oproj else "tail_ffn",
    )(*args)


def _latent_kernel(*refs, expand):
    if expand:
        (x_ref, gin_ref, wdkv_ref, glat_ref, cos_ref, sin_ref, wuk_ref, wuv_ref,
         c_ref, kr_ref, kfull_ref, v_ref) = refs
    else:
        x_ref, gin_ref, wdkv_ref, glat_ref, cos_ref, sin_ref, c_ref, kr_ref = refs
    h = _rms(x_ref[...], gin_ref[...]).astype(BF16)
    ckr = _dot(h, wdkv_ref[...])
    c = _rms(ckr[:, :KV_LORA], glat_ref[...])
    kr = ckr[:, KV_LORA:LAT_DIM] * cos_ref[...] + ckr[:, LAT_DIM:] * sin_ref[...]
    c_ref[...] = c
    kr_ref[...] = kr
    if expand:
        cb = c.astype(BF16)
        krb = kr.astype(BF16)
        kn = _dot(cb, wuk_ref[...])
        vv = _dot(cb, wuv_ref[...])
        lane = lax.broadcasted_iota(jnp.int32, (cb.shape[0], V_EXT - V_HEAD), 1)
        ones_col = jnp.where(lane == 0, 1.0, 0.0).astype(BF16)
        for hd in range(N_HEADS):
            kfull_ref[0, hd, :, 0:QK_NOPE] = kn[:, hd * QK_NOPE:(hd + 1) * QK_NOPE].astype(BF16)
            kfull_ref[0, hd, :, QK_NOPE:QK_DIM] = krb
            v_ref[0, hd, :, 0:V_HEAD] = vv[:, hd * V_HEAD:(hd + 1) * V_HEAD].astype(BF16)
            v_ref[0, hd, :, V_HEAD:V_EXT] = ones_col


def _latent_call(x, gin, wdkv, glat, cos, sin, wuk, wuv, *, tm, seq, expand):
    t = x.shape[0]
    per_seq = seq // tm
    tok = lambda w: pl.BlockSpec((tm, w), lambda i: (i, 0))
    pos = lambda w: pl.BlockSpec((tm, w), lambda i: (i % per_seq, 0))
    args = [x, gin, wdkv, glat, cos, sin]
    in_specs = [tok(D_MODEL), _const_spec(gin.shape), _const_spec(wdkv.shape),
                _const_spec(glat.shape), pos(QK_ROPE), pos(QK_ROPE)]
    out_shape = [jax.ShapeDtypeStruct((t, KV_LORA), F32), jax.ShapeDtypeStruct((t, QK_ROPE), F32)]
    out_specs = [tok(KV_LORA), tok(QK_ROPE)]
    if expand:
        args += [wuk, wuv]
        in_specs += [_const_spec(wuk.shape), _const_spec(wuv.shape)]
        nb = t // seq
        head_major = lambda w: pl.BlockSpec(
            (1, N_HEADS, tm, w), lambda i: (i // per_seq, 0, i % per_seq, 0))
        out_shape += [jax.ShapeDtypeStruct((nb, N_HEADS, seq, QK_DIM), BF16),
                      jax.ShapeDtypeStruct((nb, N_HEADS, seq, V_EXT), BF16)]
        out_specs += [head_major(QK_DIM), head_major(V_EXT)]
    return pl.pallas_call(
        functools.partial(_latent_kernel, expand=expand),
        grid=(t // tm,),
        in_specs=in_specs,
        out_specs=out_specs,
        out_shape=out_shape,
        compiler_params=_params(1),
        name="latent_expand" if expand else "latent",
    )(*args)


def _q_kernel(*refs, absorb):
    if absorb:
        x_ref, gpre_ref, wdq_ref, gq_ref, wuq_ref, cos_ref, sin_ref, wukt_ref, q_ref = refs
    else:
        x_ref, gpre_ref, wdq_ref, gq_ref, wuq_ref, cos_ref, sin_ref, q_ref = refs
    h = _rms(x_ref[...], gpre_ref[...]).astype(BF16)
    cq = _rms(_dot(h, wdq_ref[...]), gq_ref[...]).astype(BF16)
    q = _dot(cq, wuq_ref[...])
    n_nope = N_HEADS * QK_NOPE
    n_rope = N_HEADS * QK_ROPE
    qn = q[:, :n_nope]
    qr = q[:, n_nope:n_nope + n_rope] * cos_ref[...] + q[:, n_nope + n_rope:] * sin_ref[...]
    scale = SM_SCALE if absorb else SM_SCALE * LOG2E
    for hd in range(N_HEADS):
        qn_h = qn[:, hd * QK_NOPE:(hd + 1) * QK_NOPE]
        qr_h = (qr[:, hd * QK_ROPE:(hd + 1) * QK_ROPE] * scale).astype(BF16)
        if absorb:
            q_lat = _dot(qn_h.astype(BF16), wukt_ref[hd])
            q_ref[hd, :, 0:KV_LORA] = (q_lat * scale).astype(BF16)
            q_ref[hd, :, KV_LORA:LAT_DIM] = qr_h
        else:
            q_ref[0, hd, :, 0:QK_NOPE] = (qn_h * scale).astype(BF16)
            q_ref[0, hd, :, QK_NOPE:QK_DIM] = qr_h


def _q_call(x, gpre, wdq, gq, wuq, cos, sin, wukt, *, tm, seq, absorb):
    t = x.shape[0]
    per_seq = seq // tm
    width = N_HEADS * QK_ROPE
    tok = lambda w: pl.BlockSpec((tm, w), lambda i: (i, 0))
    pos = lambda w: pl.BlockSpec((tm, w), lambda i: (i % per_seq, 0))
    weights, weight_specs = zip(*map(_resident, (gpre, wdq, gq, wuq)))
    args = [x, *weights, cos, sin]
    in_specs = [tok(D_MODEL), *weight_specs, pos(width), pos(width)]
    if absorb:
        args.append(wukt)
        in_specs.append(_const_spec(wukt.shape))
        out_shape = jax.ShapeDtypeStruct((N_HEADS, t, LAT_DIM), BF16)
        out_spec = pl.BlockSpec((N_HEADS, tm, LAT_DIM), lambda i: (0, i, 0))
    else:
        out_shape = jax.ShapeDtypeStruct((t // seq, N_HEADS, seq, QK_DIM), BF16)
        out_spec = pl.BlockSpec((1, N_HEADS, tm, QK_DIM),
                                lambda i: (i // per_seq, 0, i % per_seq, 0))
    return pl.pallas_call(
        functools.partial(_q_kernel, absorb=absorb),
        grid=(t // tm,),
        in_specs=in_specs,
        out_specs=out_spec,
        out_shape=out_shape,
        compiler_params=_params(1),
        name="q_absorb" if absorb else "q_heads",
    )(*args)


def _prompt_attn_kernel(q_ref, k_ref, v_ref, o_ref, m_ref, acc_ref, *, tq, tk, hb):
    i = pl.program_id(2)
    n_full = (i * tq) // tk
    m_ref[...] = jnp.full(m_ref.shape, -jnp.inf, F32)
    acc_ref[...] = jnp.zeros(acc_ref.shape, F32)

    def step(start, mask):
        def scores(hh):
            return _dot_nt(q_ref[0, hh], k_ref[0, hh, pl.ds(start, tk), :])

        s_next = scores(0)
        for hh in range(hb):
            s = s_next
            if hh + 1 < hb:
                s_next = scores(hh + 1)
            v = v_ref[0, hh, pl.ds(start, tk), :]
            if mask is not None:
                s = jnp.where(mask, s, -jnp.inf)
            m_prev = m_ref[hh]
            m_next = jnp.maximum(m_prev, jnp.max(s, axis=-1, keepdims=True))
            alpha = jnp.exp2(m_prev - m_next)
            p = jnp.exp2(s - jnp.concatenate([m_next] * (tk // LANES), axis=1))
            acc_ref[hh] = (acc_ref[hh] * jnp.concatenate([alpha] * (V_EXT // LANES), axis=1)
                           + _dot(p.astype(BF16), v))
            m_ref[hh] = m_next

    def full_step(j, carry):
        step(pl.multiple_of(j * tk, tk), None)
        return carry

    lax.fori_loop(0, n_full, full_step, 0)
    row = lax.broadcasted_iota(jnp.int32, (tq, tk), 0)
    col = lax.broadcasted_iota(jnp.int32, (tq, tk), 1)
    for d in range(tq // tk):
        step(pl.multiple_of(i * tq + d * tk, tk), col + d * tk <= row)
    for hh in range(hb):
        acc = acc_ref[hh]
        o = acc[:, :V_HEAD] / acc[:, V_HEAD:V_HEAD + 1]
        o_ref[0, :, hh * V_HEAD:(hh + 1) * V_HEAD] = o.astype(o_ref.dtype)


def _prompt_attn_call(q, k, v, *, tq, tk, hb):
    nb, nh, seq, qk = q.shape
    assert tq % tk == 0 and seq % tq == 0 and nh % hb == 0 and k.shape[-1] == qk
    return pl.pallas_call(
        functools.partial(_prompt_attn_kernel, tq=tq, tk=tk, hb=hb),
        grid=(nb, nh // hb, seq // tq),
        in_specs=[pl.BlockSpec((1, hb, tq, qk), lambda b, h, i: (b, h, i, 0)),
                  pl.BlockSpec((1, hb, seq, qk), lambda b, h, i: (b, h, 0, 0)),
                  pl.BlockSpec((1, hb, seq, V_EXT), lambda b, h, i: (b, h, 0, 0))],
        out_specs=pl.BlockSpec((1, tq, hb * V_HEAD), lambda b, h, i: (b, i, h)),
        out_shape=jax.ShapeDtypeStruct((nb, seq, nh * V_HEAD), BF16),
        scratch_shapes=[pltpu.VMEM((hb, tq, LANES), F32),
                        pltpu.VMEM((hb, tq, V_EXT), F32)],
        compiler_params=_params(3),
        name="prompt_attn",
    )(q, k, v)


def _sample_attn_kernel(pt_ref, q_ref, cnew_ref, krnew_ref, cache_c_ref, cache_krt_ref, o_ref,
                        cbuf_ref, krbuf_ref, sem_ref, kc_ref, krt_ref,
                        *, n_pages, ch, depth, rows):
    n_rows = pl.num_programs(0) * rows
    n_chunks = n_pages // ch
    page = cbuf_ref.shape[1] // ch

    def chunk_copies(row, chunk):
        slot = chunk % depth
        copies = []
        for r in range(ch):
            pg = pt_ref[row * n_pages + chunk * ch + r]
            copies.append(pltpu.make_async_copy(
                cache_c_ref.at[pg], cbuf_ref.at[slot, pl.ds(r * page, page), :],
                sem_ref.at[slot, r]))
            copies.append(pltpu.make_async_copy(
                cache_krt_ref.at[pg], krbuf_ref.at[slot, r], sem_ref.at[slot, ch + r]))
        return copies

    @pl.when(pl.program_id(0) == 0)
    def _():
        for chunk in range(depth - 1):
            for cp in chunk_copies(0, chunk):
                cp.start()

    def attend(r_local, q, c_new, kr_new):
        b = pl.program_id(0) * rows + r_local
        q_lat = q[:, :KV_LORA]
        q_rope = q[:, KV_LORA:]

        def scores(chunk):
            ahead = chunk + depth - 1
            if ahead < n_chunks:
                for cp in chunk_copies(b, ahead):
                    cp.start()
            else:
                @pl.when(b + 1 < n_rows)
                def _():
                    for cp in chunk_copies(b + 1, ahead - n_chunks):
                        cp.start()
            for cp in chunk_copies(b, chunk):
                cp.wait()
            slot = chunk % depth
            half = chunk % 2
            kc_ref[half] = cbuf_ref[slot].astype(BF16)
            for r in range(ch):
                krt_ref[half, :, r * page:(r + 1) * page] = krbuf_ref[slot, r].astype(BF16)
            return _dot_nt(q_lat, kc_ref[half]) + _dot(q_rope, krt_ref[half])

        m = (jnp.sum(q_lat.astype(F32) * c_new, axis=-1, keepdims=True)
             + jnp.sum(q_rope.astype(F32) * kr_new, axis=-1, keepdims=True))
        l = jnp.ones_like(m)
        acc = jnp.broadcast_to(c_new, (N_HEADS, KV_LORA))

        s_next = scores(0)
        for chunk in range(n_chunks):
            s = s_next
            if chunk + 1 < n_chunks:
                s_next = scores(chunk + 1)
            m_new = jnp.maximum(m, jnp.max(s, axis=-1, keepdims=True))
            alpha = jnp.exp(m - m_new)
            p = jnp.exp(s - m_new)
            l = alpha * l + jnp.sum(p, axis=-1, keepdims=True)
            acc = alpha * acc + _dot(p.astype(BF16), kc_ref[chunk % 2])
            m = m_new
        return acc / l

    for r in range(rows):
        o_ref[r] = attend(r, q_ref[r], cnew_ref[r], krnew_ref[r]).astype(o_ref.dtype)


N_TAIL_FFN_INPUTS = 5
N_SAMPLE_ATTN_INPUTS = 5


def _ffn_sample_attn_kernel(pt_ref, *refs, n_pages, ch, depth, rows):
    tail_in = refs[:N_TAIL_FFN_INPUTS]
    sample_in = refs[N_TAIL_FFN_INPUTS:N_TAIL_FFN_INPUTS + N_SAMPLE_ATTN_INPUTS]
    x_out_ref, o_lat_ref = refs[N_TAIL_FFN_INPUTS + N_SAMPLE_ATTN_INPUTS:][:2]
    scratch = refs[N_TAIL_FFN_INPUTS + N_SAMPLE_ATTN_INPUTS + 2:]
    _tail_kernel(*tail_in, x_out_ref, has_oproj=False)
    _sample_attn_kernel(pt_ref, *sample_in, o_lat_ref, *scratch,
                        n_pages=n_pages, ch=ch, depth=depth, rows=rows)


def _ffn_sample_attn_call(x, gpf, wup, wdn, gpo, page_table, q, c_new, kr_new, cache_c,
                          cache_krt, *, ch, depth, rows):
    t = x.shape[0]
    nb, n_pages = page_table.shape
    page = cache_c.shape[1]
    n_chunks = n_pages // ch
    n_steps = nb // rows
    tm = t // n_steps
    assert nb % rows == 0 and t % n_steps == 0 and tm % 8 == 0
    assert n_pages % ch == 0 and n_chunks % depth == 0 and depth >= 2
    tok = lambda w: pl.BlockSpec((tm, w), lambda b, pt: (b, 0))
    row = lambda h, w: pl.BlockSpec((rows, h, w), lambda b, pt: (b, 0, 0))
    weights, weight_specs = zip(*map(_resident, (gpf, wup, wdn, gpo)))
    return pl.pallas_call(
        functools.partial(_ffn_sample_attn_kernel, n_pages=n_pages, ch=ch, depth=depth,
                          rows=rows),
        grid_spec=pltpu.PrefetchScalarGridSpec(
            num_scalar_prefetch=1,
            grid=(n_steps,),
            in_specs=[tok(D_MODEL), *weight_specs,
                      row(N_HEADS, LAT_DIM), row(1, KV_LORA), row(1, QK_ROPE),
                      pl.BlockSpec(memory_space=pl.ANY), pl.BlockSpec(memory_space=pl.ANY)],
            out_specs=[tok(D_MODEL), row(N_HEADS, KV_LORA)],
            scratch_shapes=[pltpu.VMEM((depth, ch * page, KV_LORA), F32),
                            pltpu.VMEM((depth, ch, QK_ROPE, page), F32),
                            pltpu.SemaphoreType.DMA((depth, 2 * ch)),
                            pltpu.VMEM((2, ch * page, KV_LORA), BF16),
                            pltpu.VMEM((2, QK_ROPE, ch * page), BF16)],
        ),
        out_shape=[jax.ShapeDtypeStruct((t, D_MODEL), F32),
                   jax.ShapeDtypeStruct((nb, N_HEADS, KV_LORA), BF16)],
        compiler_params=_params(1),
        name="ffn_sample_attn",
    )(page_table.reshape(-1), x, *weights, q, c_new.reshape(nb, 1, KV_LORA),
      kr_new.reshape(nb, 1, QK_ROPE), cache_c, cache_krt)


def _uv_kernel(o_ref, wuv_ref, out_ref):
    for hd in range(N_HEADS):
        out_ref[:, hd * V_HEAD:(hd + 1) * V_HEAD] = _dot(o_ref[hd], wuv_ref[hd]).astype(out_ref.dtype)


def _uv_call(o_lat, wuv_heads):
    nh, t, _ = o_lat.shape
    return pl.pallas_call(
        _uv_kernel,
        out_shape=jax.ShapeDtypeStruct((t, nh * V_HEAD), BF16),
        compiler_params=pltpu.CompilerParams(vmem_limit_bytes=VMEM_LIMIT_BYTES),
        name="uv_proj",
    )(o_lat, wuv_heads)


def _rope_tables(pos, reps):
    half = QK_ROPE // 2
    inv = ROPE_BASE ** (-jnp.arange(half, dtype=F32) / half)
    ang = pos.astype(F32)[:, None] * inv[None, :]
    cos = jnp.cos(ang)
    sin = jnp.sin(ang)
    cos64 = jnp.concatenate([cos, cos], axis=1)
    sin64 = jnp.concatenate([-sin, sin], axis=1)
    return cos64, sin64, jnp.tile(cos64, (1, reps)), jnp.tile(sin64, (1, reps))


def _row(v):
    return v.reshape(1, -1).astype(F32)


@jax.jit
def _forward(x_prompt, x_sample, cache_kv_latent, cache_k_rope, page_table,
             norm_pre_mix, norm_post_mix, norm_pre_ffn, norm_post_ffn,
             a_w_in, a_ln_g, a_ln_b, a_w_s, a_b_s, a_w_out,
             kv_norm_in, w_dkv, kv_latent_norm, w_uk, w_uv,
             b_w_dq, b_q_norm, b_w_uq, b_w_o, ffn_w_up, ffn_w_down):
    nb, seq, _ = x_prompt.shape
    db, dseq, _ = x_sample.shape
    n_pages = page_table.shape[1]
    n_past = n_pages * cache_kv_latent.shape[1]

    bf = lambda w: w.astype(BF16)
    w_in = bf(a_w_in)
    w_out = bf(a_w_out)
    w_up = bf(ffn_w_up)
    w_down = bf(ffn_w_down)
    half = QK_ROPE // 2
    swap_halves = lambda w: jnp.concatenate([w[..., half:], w[..., :half]], axis=-1)
    w_dkv_b = bf(jnp.concatenate([w_dkv, swap_halves(w_dkv[:, KV_LORA:])], axis=1))
    w_uk2d = bf(w_uk.reshape(KV_LORA, N_HEADS * QK_NOPE))
    w_uv2d = bf(w_uv.reshape(KV_LORA, N_HEADS * V_HEAD))
    w_ukt = bf(jnp.transpose(w_uk, (1, 2, 0)))
    w_uv_heads = bf(jnp.transpose(w_uv, (1, 0, 2)))
    w_dq = bf(b_w_dq)
    w_o = bf(b_w_o)
    cache_krt = jnp.transpose(cache_k_rope, (0, 2, 1))
    w_uq4 = b_w_uq.reshape(N_B_LAYERS, Q_LORA, N_HEADS, QK_DIM)
    w_uq_rope = w_uq4[..., QK_NOPE:]
    w_uq = bf(jnp.concatenate(
        [w_uq4[..., :QK_NOPE].reshape(N_B_LAYERS, Q_LORA, N_HEADS * QK_NOPE),
         w_uq_rope.reshape(N_B_LAYERS, Q_LORA, N_HEADS * QK_ROPE),
         swap_halves(w_uq_rope).reshape(N_B_LAYERS, Q_LORA, N_HEADS * QK_ROPE)], axis=-1))

    bias_chunk = jnp.repeat(jnp.transpose(a_b_s, (0, 2, 1)), GROUP_DIM, axis=2)
    bias_single = jnp.repeat(a_b_s[:, :, 0], GROUP_DIM, axis=1)[:, None, :]
    w_single = jnp.repeat(a_w_s[:, :, 0, 0], GROUP_DIM, axis=1)[:, None, :]

    def mixer(x, l, *, tm, single):
        return _mixer_call(
            x, _row(norm_pre_mix[l]), (w_in, l), _row(a_ln_g[l]), _row(a_ln_b[l]),
            (w_single if single else a_w_s, l), (bias_single if single else bias_chunk, l),
            (w_out, l), _row(norm_post_mix[l]), tm=tm, single=single)

    def ffn_weights(l):
        return _row(norm_pre_ffn[l]), (w_up, l), (w_down, l), _row(norm_post_ffn[l])

    def tail(x, l, attn, *, tm):
        if attn is None:
            return _tail_call(x, None, None, None, *ffn_weights(l), tm=tm)
        return _tail_call(x, attn, (w_o, l - N_A_LAYERS), _row(norm_post_mix[l]),
                          *ffn_weights(l), tm=tm)

    def queries(x, l, tables, *, tm, seq_len, absorb):
        j = l - N_A_LAYERS
        return _q_call(x, _row(norm_pre_mix[l]), (w_dq, j), _row(b_q_norm[j]), (w_uq, j),
                       tables[2], tables[3], w_ukt, tm=tm, seq=seq_len, absorb=absorb)

    def latent(x, tables, *, tm, seq_len, expand):
        return _latent_call(x, _row(kv_norm_in), w_dkv_b, _row(kv_latent_norm), tables[0],
                            tables[1], w_uk2d, w_uv2d, tm=tm, seq=seq_len, expand=expand)

    assert N_B_LAYERS <= N_A_LAYERS
    ts = db * dseq
    tp = PROMPT_TOKEN_TILE
    tables_s = _rope_tables(jnp.full((ts,), n_past, dtype=jnp.int32), N_HEADS)
    tables_p = _rope_tables(jnp.arange(seq, dtype=jnp.int32), N_HEADS)

    xs = x_sample.reshape(ts, D_MODEL)
    v_rows = []
    for l in range(N_A_LAYERS):
        xs, v = mixer(xs, l, tm=ts, single=True)
        v_rows.append(v)
        xs = tail(xs, l, None, tm=ts)
    c_s, kr_s = latent(xs, tables_s, tm=ts, seq_len=ts, expand=False)
    q_s = queries(xs, N_A_LAYERS, tables_s, tm=ts, seq_len=ts, absorb=True)

    xp = x_prompt.reshape(nb * seq, D_MODEL)
    for l in range(N_A_LAYERS):
        (xp,) = mixer(xp, l, tm=tp, single=False)
        if l >= N_B_LAYERS:
            xp = tail(xp, l, None, tm=tp)
            continue
        lb = N_A_LAYERS + l
        xp, o_lat = _ffn_sample_attn_call(
            xp, *ffn_weights(l), page_table, jnp.transpose(q_s, (1, 0, 2)), c_s, kr_s,
            cache_kv_latent, cache_krt, ch=PAGES_PER_CHUNK, depth=PAGE_RING_DEPTH,
            rows=SAMPLE_ROWS_PER_STEP)
        attn_s = _uv_call(jnp.transpose(o_lat, (1, 0, 2)), w_uv_heads)
        xs = tail(xs, lb, attn_s, tm=ts)
        if l + 1 < N_B_LAYERS:
            q_s = queries(xs, lb + 1, tables_s, tm=ts, seq_len=ts, absorb=True)

    c_p, kr_p, kfull, vfull = latent(xp, tables_p, tm=tp, seq_len=seq, expand=True)
    for l in range(N_A_LAYERS, N_A_LAYERS + N_B_LAYERS):
        q_p = queries(xp, l, tables_p, tm=tp, seq_len=seq, absorb=False)
        attn_p = _prompt_attn_call(q_p, kfull, vfull, tq=ATTN_Q_TILE, tk=ATTN_KV_TILE,
                                   hb=ATTN_HEADS_PER_STEP)
        xp = tail(xp, l, attn_p.reshape(nb * seq, N_HEADS * V_HEAD), tm=tp)

    gate_v = jnp.stack(v_rows, axis=0).reshape(N_A_LAYERS, db, dseq, D_GATE)
    return (xp.reshape(nb, seq, D_MODEL), xs.reshape(db, dseq, D_MODEL),
            c_p.reshape(nb, seq, KV_LORA), kr_p.reshape(nb, seq, QK_ROPE),
            c_s.reshape(db, dseq, KV_LORA), kr_s.reshape(db, dseq, QK_ROPE), gate_v)


def kernel(x_prompt, x_sample, cache_kv_latent, cache_k_rope, page_table, norm_pre_mix, norm_post_mix, norm_pre_ffn, norm_post_ffn, a_w_in, a_ln_g, a_ln_b, a_w_s, a_b_s, a_w_out, kv_norm_in, w_dkv, kv_latent_norm, w_uk, w_uv, b_w_dq, b_q_norm, b_w_uq, b_w_o, ffn_w_up, ffn_w_down):
    assert x_sample.shape[1] == 1, "sample path assumes one new token per row"
    return _forward(x_prompt, x_sample, cache_kv_latent, cache_k_rope, page_table,
                    norm_pre_mix, norm_post_mix, norm_pre_ffn, norm_post_ffn,
                    a_w_in, a_ln_g, a_ln_b, a_w_s, a_b_s, a_w_out,
                    kv_norm_in, w_dkv, kv_latent_norm, w_uk, w_uv,
                    b_w_dq, b_q_norm, b_w_uq, b_w_o, ffn_w_up, ffn_w_down)
```

```python
import functools

import jax
import jax.numpy as jnp
from jax import lax
from jax.experimental import pallas as pl
from jax.experimental.pallas import tpu as pltpu

D_MODEL = 1024
CHUNK = 128
D_GATE = 2 * D_MODEL
N_GROUPS = 8
GROUP_DIM = D_GATE // N_GROUPS
D_FF = 4 * D_MODEL
N_HEADS = 16
QK_NOPE = 128
QK_ROPE = 64
QK_DIM = QK_NOPE + QK_ROPE
V_HEAD = 128
Q_LORA = 256
KV_LORA = 512
LAT_DIM = KV_LORA + QK_ROPE
ROPE_BASE = 10000.0
EPS = 1e-6
SM_SCALE = QK_DIM ** -0.5
N_A_LAYERS = 2
N_B_LAYERS = 2

F32 = jnp.float32
BF16 = jnp.bfloat16

VMEM_LIMIT_BYTES = 56 * 1024 * 1024

LANES = 128
V_EXT = 2 * LANES
LOG2E = 1.4426950408889634

PROMPT_TOKEN_TILE = 512
ATTN_Q_TILE = 512
ATTN_KV_TILE = 512
ATTN_HEADS_PER_STEP = 8
PAGES_PER_CHUNK = 16
PAGE_RING_DEPTH = 4
SAMPLE_ROWS_PER_STEP = 1
PAGE_DMA_PRIORITY = 1


def _dot(a, b):
    return jnp.dot(a, b, preferred_element_type=F32)


def _dot_nt(a, b):
    return lax.dot_general(a, b, (((1,), (1,)), ((), ())), preferred_element_type=F32)


def _rms(x, g):
    return x * lax.rsqrt(jnp.mean(x * x, axis=-1, keepdims=True) + EPS) * g


def _gelu(z):
    return 0.5 * z * (1.0 + lax.erf(z * (0.5 ** 0.5)))


def _const_spec(shape):
    n = len(shape)
    return pl.BlockSpec(shape, lambda *_: (0,) * n, pipeline_mode=pl.Buffered(1))


def _resident(w):
    if isinstance(w, tuple):
        arr, layer = w
        n = arr.ndim - 1
        return arr, pl.BlockSpec((None,) + arr.shape[1:], lambda *_: (layer,) + (0,) * n,
                                 pipeline_mode=pl.Buffered(1))
    return w, _const_spec(w.shape)


def _params(n_grid):
    return pltpu.CompilerParams(
        dimension_semantics=("arbitrary",) * n_grid,
        vmem_limit_bytes=VMEM_LIMIT_BYTES,
    )


def _mixer_kernel(x_ref, gpre_ref, win_ref, lng_ref, lnb_ref, ws_ref, bias_ref, wout_ref,
                  gpost_ref, *rest, tm, single):
    if single:
        o_ref, v_ref, gated_ref = rest
    else:
        o_ref, gated_ref = rest
    x = x_ref[...]
    h = _rms(x, gpre_ref[...]).astype(BF16)
    v = _gelu(_dot(h, win_ref[:, D_GATE:]))
    u = _gelu(_dot(h, win_ref[:, :D_GATE]))
    mu = jnp.mean(v, axis=-1, keepdims=True)
    vc = v - mu
    v = vc * lax.rsqrt(jnp.mean(vc * vc, axis=-1, keepdims=True) + EPS) * lng_ref[...] + lnb_ref[...]
    if single:
        v_ref[...] = v
        gated_ref[...] = (u * (v * ws_ref[...] + bias_ref[...])).astype(BF16)
    else:
        vb = v.astype(BF16)
        row = lax.broadcasted_iota(jnp.int32, (CHUNK, CHUNK), 0)
        col = lax.broadcasted_iota(jnp.int32, (CHUNK, CHUNK), 1)
        causal = row >= col
        for g in range(N_GROUPS):
            wg = jnp.where(causal, ws_ref[g], 0.0).astype(BF16)
            cols = slice(g * GROUP_DIM, (g + 1) * GROUP_DIM)
            for c in range(tm // CHUNK):
                rows = slice(c * CHUNK, (c + 1) * CHUNK)
                s = _dot(wg, vb[rows, cols]) + bias_ref[:, cols]
                gated_ref[rows, cols] = (u[rows, cols] * s).astype(BF16)
    m = _dot(gated_ref[...], wout_ref[...])
    o_ref[...] = x + _rms(m, gpost_ref[...])


def _mixer_call(x, gpre, win, lng, lnb, ws, bias, wout, gpost, *, tm, single):
    t = x.shape[0]
    tok = lambda w: pl.BlockSpec((tm, w), lambda i: (i, 0))
    out_shape = [jax.ShapeDtypeStruct((t, D_MODEL), F32)]
    out_specs = [tok(D_MODEL)]
    if single:
        out_shape.append(jax.ShapeDtypeStruct((t, D_GATE), F32))
        out_specs.append(tok(D_GATE))
    weights, weight_specs = zip(*map(_resident, (gpre, win, lng, lnb, ws, bias, wout, gpost)))
    return pl.pallas_call(
        functools.partial(_mixer_kernel, tm=tm, single=single),
        grid=(t // tm,),
        in_specs=[tok(D_MODEL), *weight_specs],
        out_specs=out_specs,
        out_shape=out_shape,
        scratch_shapes=[pltpu.VMEM((tm, D_GATE), BF16)],
        compiler_params=_params(1),
        name="mixer_single" if single else "mixer_chunk",
    )(x, *weights)


def _tail_kernel(*refs, has_oproj):
    if has_oproj:
        x_ref, a_ref, wo_ref, gpm_ref, gpf_ref, wup_ref, wdn_ref, gpo_ref, o_ref = refs
    else:
        x_ref, gpf_ref, wup_ref, wdn_ref, gpo_ref, o_ref = refs
    x = x_ref[...]
    if has_oproj:
        x = x + _rms(_dot(a_ref[...], wo_ref[...]), gpm_ref[...])
    h = _rms(x, gpf_ref[...]).astype(BF16)
    a = jnp.square(jnp.maximum(_dot(h, wup_ref[...]), 0.0)).astype(BF16)
    o_ref[...] = x + _rms(_dot(a, wdn_ref[...]), gpo_ref[...])


def _tail_call(x, attn, wo, gpm, gpf, wup, wdn, gpo, *, tm):
    t = x.shape[0]
    tok = lambda w: pl.BlockSpec((tm, w), lambda i: (i, 0))
    has_oproj = attn is not None
    if has_oproj:
        weights, weight_specs = zip(*map(_resident, (wo, gpm, gpf, wup, wdn, gpo)))
        args = (x, attn, *weights)
        in_specs = [tok(D_MODEL), tok(attn.shape[1]), *weight_specs]
    else:
        weights, weight_specs = zip(*map(_resident, (gpf, wup, wdn, gpo)))
        args = (x, *weights)
        in_specs = [tok(D_MODEL), *weight_specs]
    return pl.pallas_call(
        functools.partial(_tail_kernel, has_oproj=has_oproj),
        grid=(t // tm,),
        in_specs=in_specs,
        out_specs=tok(D_MODEL),
        out_shape=jax.ShapeDtypeStruct((t, D_MODEL), F32),
        compiler_params=_params(1),
        name="tail_oproj" if has_oproj else "tail_ffn",
    )(*args)


def _latent_kernel(*refs, expand):
    if expand:
        (x_ref, gin_ref, wdkv_ref, glat_ref, cos_ref, sin_ref, wuk_ref, wuv_ref,
         c_ref, kr_ref, kfull_ref, v_ref) = refs
    else:
        x_ref, gin_ref, wdkv_ref, glat_ref, cos_ref, sin_ref, c_ref, kr_ref = refs
    h = _rms(x_ref[...], gin_ref[...]).astype(BF16)
    ckr = _dot(h, wdkv_ref[...])
    c = _rms(ckr[:, :KV_LORA], glat_ref[...])
    kr = ckr[:, KV_LORA:LAT_DIM] * cos_ref[...] + ckr[:, LAT_DIM:] * sin_ref[...]
    c_ref[...] = c
    kr_ref[...] = kr
    if expand:
        cb = c.astype(BF16)
        krb = kr.astype(BF16)
        kn = _dot(cb, wuk_ref[...])
        vv = _dot(cb, wuv_ref[...])
        lane = lax.broadcasted_iota(jnp.int32, (cb.shape[0], V_EXT - V_HEAD), 1)
        ones_col = jnp.where(lane == 0, 1.0, 0.0).astype(BF16)
        for hd in range(N_HEADS):
            kfull_ref[0, hd, :, 0:QK_NOPE] = kn[:, hd * QK_NOPE:(hd + 1) * QK_NOPE].astype(BF16)
            kfull_ref[0, hd, :, QK_NOPE:QK_DIM] = krb
            v_ref[0, hd, :, 0:V_HEAD] = vv[:, hd * V_HEAD:(hd + 1) * V_HEAD].astype(BF16)
            v_ref[0, hd, :, V_HEAD:V_EXT] = ones_col


def _latent_call(x, gin, wdkv, glat, cos, sin, wuk, wuv, *, tm, seq, expand):
    t = x.shape[0]
    per_seq = seq // tm
    tok = lambda w: pl.BlockSpec((tm, w), lambda i: (i, 0))
    pos = lambda w: pl.BlockSpec((tm, w), lambda i: (i % per_seq, 0))
    args = [x, gin, wdkv, glat, cos, sin]
    in_specs = [tok(D_MODEL), _const_spec(gin.shape), _const_spec(wdkv.shape),
                _const_spec(glat.shape), pos(QK_ROPE), pos(QK_ROPE)]
    out_shape = [jax.ShapeDtypeStruct((t, KV_LORA), F32), jax.ShapeDtypeStruct((t, QK_ROPE), F32)]
    out_specs = [tok(KV_LORA), tok(QK_ROPE)]
    if expand:
        args += [wuk, wuv]
        in_specs += [_const_spec(wuk.shape), _const_spec(wuv.shape)]
        nb = t // seq
        head_major = lambda w: pl.BlockSpec(
            (1, N_HEADS, tm, w), lambda i: (i // per_seq, 0, i % per_seq, 0))
        out_shape += [jax.ShapeDtypeStruct((nb, N_HEADS, seq, QK_DIM), BF16),
                      jax.ShapeDtypeStruct((nb, N_HEADS, seq, V_EXT), BF16)]
        out_specs += [head_major(QK_DIM), head_major(V_EXT)]
    return pl.pallas_call(
        functools.partial(_latent_kernel, expand=expand),
        grid=(t // tm,),
        in_specs=in_specs,
        out_specs=out_specs,
        out_shape=out_shape,
        compiler_params=_params(1),
        name="latent_expand" if expand else "latent",
    )(*args)


def _q_kernel(*refs, absorb):
    if absorb:
        x_ref, gpre_ref, wdq_ref, gq_ref, wuq_ref, cos_ref, sin_ref, wukt_ref, q_ref = refs
    else:
        x_ref, gpre_ref, wdq_ref, gq_ref, wuq_ref, cos_ref, sin_ref, q_ref = refs
    h = _rms(x_ref[...], gpre_ref[...]).astype(BF16)
    cq = _rms(_dot(h, wdq_ref[...]), gq_ref[...]).astype(BF16)
    q = _dot(cq, wuq_ref[...])
    n_nope = N_HEADS * QK_NOPE
    n_rope = N_HEADS * QK_ROPE
    qn = q[:, :n_nope]
    qr = q[:, n_nope:n_nope + n_rope] * cos_ref[...] + q[:, n_nope + n_rope:] * sin_ref[...]
    scale = SM_SCALE if absorb else SM_SCALE * LOG2E
    for hd in range(N_HEADS):
        qn_h = qn[:, hd * QK_NOPE:(hd + 1) * QK_NOPE]
        qr_h = (qr[:, hd * QK_ROPE:(hd + 1) * QK_ROPE] * scale).astype(BF16)
        if absorb:
            q_lat = _dot(qn_h.astype(BF16), wukt_ref[hd])
            q_ref[hd, :, 0:KV_LORA] = (q_lat * scale).astype(BF16)
            q_ref[hd, :, KV_LORA:LAT_DIM] = qr_h
        else:
            q_ref[0, hd, :, 0:QK_NOPE] = (qn_h * scale).astype(BF16)
            q_ref[0, hd, :, QK_NOPE:QK_DIM] = qr_h


def _q_call(x, gpre, wdq, gq, wuq, cos, sin, wukt, *, tm, seq, absorb):
    t = x.shape[0]
    per_seq = seq // tm
    width = N_HEADS * QK_ROPE
    tok = lambda w: pl.BlockSpec((tm, w), lambda i: (i, 0))
    pos = lambda w: pl.BlockSpec((tm, w), lambda i: (i % per_seq, 0))
    weights, weight_specs = zip(*map(_resident, (gpre, wdq, gq, wuq)))
    args = [x, *weights, cos, sin]
    in_specs = [tok(D_MODEL), *weight_specs, pos(width), pos(width)]
    if absorb:
        args.append(wukt)
        in_specs.append(_const_spec(wukt.shape))
        out_shape = jax.ShapeDtypeStruct((N_HEADS, t, LAT_DIM), BF16)
        out_spec = pl.BlockSpec((N_HEADS, tm, LAT_DIM), lambda i: (0, i, 0))
    else:
        out_shape = jax.ShapeDtypeStruct((t // seq, N_HEADS, seq, QK_DIM), BF16)
        out_spec = pl.BlockSpec((1, N_HEADS, tm, QK_DIM),
                                lambda i: (i // per_seq, 0, i % per_seq, 0))
    return pl.pallas_call(
        functools.partial(_q_kernel, absorb=absorb),
        grid=(t // tm,),
        in_specs=in_specs,
        out_specs=out_spec,
        out_shape=out_shape,
        compiler_params=_params(1),
        name="q_absorb" if absorb else "q_heads",
    )(*args)


def _prompt_attn_kernel(q_ref, k_ref, v_ref, o_ref, m_ref, acc_ref, *, tq, tk, hb):
    i = pl.program_id(2)
    n_full = (i * tq) // tk
    m_ref[...] = jnp.full(m_ref.shape, -jnp.inf, F32)
    acc_ref[...] = jnp.zeros(acc_ref.shape, F32)

    def step(start, mask):
        def scores(hh):
            return _dot_nt(q_ref[0, hh], k_ref[0, hh, pl.ds(start, tk), :])

        s_next = scores(0)
        for hh in range(hb):
            s = s_next
            if hh + 1 < hb:
                s_next = scores(hh + 1)
            v = v_ref[0, hh, pl.ds(start, tk), :]
            if mask is not None:
                s = jnp.where(mask, s, -jnp.inf)
            m_prev = m_ref[hh]
            m_next = jnp.maximum(m_prev, jnp.max(s, axis=-1, keepdims=True))
            alpha = jnp.exp2(m_prev - m_next)
            p = jnp.exp2(s - jnp.concatenate([m_next] * (tk // LANES), axis=1))
            acc_ref[hh] = (acc_ref[hh] * jnp.concatenate([alpha] * (V_EXT // LANES), axis=1)
                           + _dot(p.astype(BF16), v))
            m_ref[hh] = m_next

    def full_step(j, carry):
        step(pl.multiple_of(j * tk, tk), None)
        return carry

    lax.fori_loop(0, n_full, full_step, 0)
    row = lax.broadcasted_iota(jnp.int32, (tq, tk), 0)
    col = lax.broadcasted_iota(jnp.int32, (tq, tk), 1)
    for d in range(tq // tk):
        step(pl.multiple_of(i * tq + d * tk, tk), col + d * tk <= row)
    for hh in range(hb):
        acc = acc_ref[hh]
        o = acc[:, :V_HEAD] / acc[:, V_HEAD:V_HEAD + 1]
        o_ref[0, :, hh * V_HEAD:(hh + 1) * V_HEAD] = o.astype(o_ref.dtype)


def _prompt_attn_call(q, k, v, *, tq, tk, hb):
    nb, nh, seq, qk = q.shape
    assert tq % tk == 0 and seq % tq == 0 and nh % hb == 0 and k.shape[-1] == qk
    return pl.pallas_call(
        functools.partial(_prompt_attn_kernel, tq=tq, tk=tk, hb=hb),
        grid=(nb, nh // hb, seq // tq),
        in_specs=[pl.BlockSpec((1, hb, tq, qk), lambda b, h, i: (b, h, i, 0)),
                  pl.BlockSpec((1, hb, seq, qk), lambda b, h, i: (b, h, 0, 0)),
                  pl.BlockSpec((1, hb, seq, V_EXT), lambda b, h, i: (b, h, 0, 0))],
        out_specs=pl.BlockSpec((1, tq, hb * V_HEAD), lambda b, h, i: (b, i, h)),
        out_shape=jax.ShapeDtypeStruct((nb, seq, nh * V_HEAD), BF16),
        scratch_shapes=[pltpu.VMEM((hb, tq, LANES), F32),
                        pltpu.VMEM((hb, tq, V_EXT), F32)],
        compiler_params=_params(3),
        name="prompt_attn",
    )(q, k, v)


def _sample_attn_kernel(pt_ref, q_ref, cnew_ref, krnew_ref, cache_c_ref, cache_krt_ref, o_ref,
                        cbuf_ref, krbuf_ref, sem_ref, kc_ref, krt_ref,
                        *, n_pages, ch, depth, rows):
    n_rows = pl.num_programs(0) * rows
    n_chunks = n_pages // ch
    page = cbuf_ref.shape[1] // ch

    def chunk_copies(row, chunk):
        slot = chunk % depth
        copies = []
        for r in range(ch):
            pg = pt_ref[row * n_pages + chunk * ch + r]
            copies.append(pltpu.make_async_copy(
                cache_c_ref.at[pg], cbuf_ref.at[slot, pl.ds(r * page, page), :],
                sem_ref.at[slot, r]))
            copies.append(pltpu.make_async_copy(
                cache_krt_ref.at[pg], krbuf_ref.at[slot, r], sem_ref.at[slot, ch + r]))
        return copies

    @pl.when(pl.program_id(0) == 0)
    def _():
        for chunk in range(depth - 1):
            for cp in chunk_copies(0, chunk):
                cp.start(priority=PAGE_DMA_PRIORITY)

    def attend(r_local, q, c_new, kr_new):
        b = pl.program_id(0) * rows + r_local
        q_lat = q[:, :KV_LORA]
        q_rope = q[:, KV_LORA:]

        def scores(chunk):
            ahead = chunk + depth - 1
            if ahead < n_chunks:
                for cp in chunk_copies(b, ahead):
                    cp.start(priority=PAGE_DMA_PRIORITY)
            else:
                @pl.when(b + 1 < n_rows)
                def _():
                    for cp in chunk_copies(b + 1, ahead - n_chunks):
                        cp.start(priority=PAGE_DMA_PRIORITY)
            for cp in chunk_copies(b, chunk):
                cp.wait()
            slot = chunk % depth
            half = chunk % 2
            kc_ref[half] = cbuf_ref[slot].astype(BF16)
            for r in range(ch):
                krt_ref[half, :, r * page:(r + 1) * page] = krbuf_ref[slot, r].astype(BF16)
            return _dot_nt(q_lat, kc_ref[half]) + _dot(q_rope, krt_ref[half])

        m = (jnp.sum(q_lat.astype(F32) * c_new, axis=-1, keepdims=True)
             + jnp.sum(q_rope.astype(F32) * kr_new, axis=-1, keepdims=True))
        l = jnp.ones_like(m)
        acc = jnp.broadcast_to(c_new, (N_HEADS, KV_LORA))

        s_next = scores(0)
        for chunk in range(n_chunks):
            s = s_next
            if chunk + 1 < n_chunks:
                s_next = scores(chunk + 1)
            m_new = jnp.maximum(m, jnp.max(s, axis=-1, keepdims=True))
            alpha = jnp.exp(m - m_new)
            p = jnp.exp(s - m_new)
            l = alpha * l + jnp.sum(p, axis=-1, keepdims=True)
            acc = alpha * acc + _dot(p.astype(BF16), kc_ref[chunk % 2])
            m = m_new
        return acc / l

    for r in range(rows):
        o_ref[r] = attend(r, q_ref[r], cnew_ref[r], krnew_ref[r]).astype(o_ref.dtype)


N_TAIL_FFN_INPUTS = 5
N_SAMPLE_ATTN_INPUTS = 5


def _ffn_sample_attn_kernel(pt_ref, *refs, n_pages, ch, depth, rows):
    tail_in = refs[:N_TAIL_FFN_INPUTS]
    sample_in = refs[N_TAIL_FFN_INPUTS:N_TAIL_FFN_INPUTS + N_SAMPLE_ATTN_INPUTS]
    x_out_ref, o_lat_ref = refs[N_TAIL_FFN_INPUTS + N_SAMPLE_ATTN_INPUTS:][:2]
    scratch = refs[N_TAIL_FFN_INPUTS + N_SAMPLE_ATTN_INPUTS + 2:]
    _tail_kernel(*tail_in, x_out_ref, has_oproj=False)
    _sample_attn_kernel(pt_ref, *sample_in, o_lat_ref, *scratch,
                        n_pages=n_pages, ch=ch, depth=depth, rows=rows)


def _ffn_sample_attn_call(x, gpf, wup, wdn, gpo, page_table, q, c_new, kr_new, cache_c,
                          cache_krt, *, ch, depth, rows):
    t = x.shape[0]
    nb, n_pages = page_table.shape
    page = cache_c.shape[1]
    n_chunks = n_pages // ch
    n_steps = nb // rows
    tm = t // n_steps
    assert nb % rows == 0 and t % n_steps == 0 and tm % 8 == 0
    assert n_pages % ch == 0 and n_chunks % depth == 0 and depth >= 2
    tok = lambda w: pl.BlockSpec((tm, w), lambda b, pt: (b, 0))
    row = lambda h, w: pl.BlockSpec((rows, h, w), lambda b, pt: (b, 0, 0))
    weights, weight_specs = zip(*map(_resident, (gpf, wup, wdn, gpo)))
    return pl.pallas_call(
        functools.partial(_ffn_sample_attn_kernel, n_pages=n_pages, ch=ch, depth=depth,
                          rows=rows),
        grid_spec=pltpu.PrefetchScalarGridSpec(
            num_scalar_prefetch=1,
            grid=(n_steps,),
            in_specs=[tok(D_MODEL), *weight_specs,
                      row(N_HEADS, LAT_DIM), row(1, KV_LORA), row(1, QK_ROPE),
                      pl.BlockSpec(memory_space=pl.ANY), pl.BlockSpec(memory_space=pl.ANY)],
            out_specs=[tok(D_MODEL), row(N_HEADS, KV_LORA)],
            scratch_shapes=[pltpu.VMEM((depth, ch * page, KV_LORA), F32),
                            pltpu.VMEM((depth, ch, QK_ROPE, page), F32),
                            pltpu.SemaphoreType.DMA((depth, 2 * ch)),
                            pltpu.VMEM((2, ch * page, KV_LORA), BF16),
                            pltpu.VMEM((2, QK_ROPE, ch * page), BF16)],
        ),
        out_shape=[jax.ShapeDtypeStruct((t, D_MODEL), F32),
                   jax.ShapeDtypeStruct((nb, N_HEADS, KV_LORA), BF16)],
        compiler_params=_params(1),
        name="ffn_sample_attn",
    )(page_table.reshape(-1), x, *weights, q, c_new.reshape(nb, 1, KV_LORA),
      kr_new.reshape(nb, 1, QK_ROPE), cache_c, cache_krt)


def _uv_kernel(o_ref, wuv_ref, out_ref):
    for hd in range(N_HEADS):
        out_ref[:, hd * V_HEAD:(hd + 1) * V_HEAD] = _dot(o_ref[hd], wuv_ref[hd]).astype(out_ref.dtype)


def _uv_call(o_lat, wuv_heads):
    nh, t, _ = o_lat.shape
    return pl.pallas_call(
        _uv_kernel,
        out_shape=jax.ShapeDtypeStruct((t, nh * V_HEAD), BF16),
        compiler_params=pltpu.CompilerParams(vmem_limit_bytes=VMEM_LIMIT_BYTES),
        name="uv_proj",
    )(o_lat, wuv_heads)


def _rope_tables(pos, reps):
    half = QK_ROPE // 2
    inv = ROPE_BASE ** (-jnp.arange(half, dtype=F32) / half)
    ang = pos.astype(F32)[:, None] * inv[None, :]
    cos = jnp.cos(ang)
    sin = jnp.sin(ang)
    cos64 = jnp.concatenate([cos, cos], axis=1)
    sin64 = jnp.concatenate([-sin, sin], axis=1)
    return cos64, sin64, jnp.tile(cos64, (1, reps)), jnp.tile(sin64, (1, reps))


def _row(v):
    return v.reshape(1, -1).astype(F32)


@jax.jit
def _forward(x_prompt, x_sample, cache_kv_latent, cache_k_rope, page_table,
             norm_pre_mix, norm_post_mix, norm_pre_ffn, norm_post_ffn,
             a_w_in, a_ln_g, a_ln_b, a_w_s, a_b_s, a_w_out,
             kv_norm_in, w_dkv, kv_latent_norm, w_uk, w_uv,
             b_w_dq, b_q_norm, b_w_uq, b_w_o, ffn_w_up, ffn_w_down):
    nb, seq, _ = x_prompt.shape
    db, dseq, _ = x_sample.shape
    n_pages = page_table.shape[1]
    n_past = n_pages * cache_kv_latent.shape[1]

    bf = lambda w: w.astype(BF16)
    w_in = bf(a_w_in)
    w_out = bf(a_w_out)
    w_up = bf(ffn_w_up)
    w_down = bf(ffn_w_down)
    half = QK_ROPE // 2
    swap_halves = lambda w: jnp.concatenate([w[..., half:], w[..., :half]], axis=-1)
    w_dkv_b = bf(jnp.concatenate([w_dkv, swap_halves(w_dkv[:, KV_LORA:])], axis=1))
    w_uk2d = bf(w_uk.reshape(KV_LORA, N_HEADS * QK_NOPE))
    w_uv2d = bf(w_uv.reshape(KV_LORA, N_HEADS * V_HEAD))
    w_ukt = bf(jnp.transpose(w_uk, (1, 2, 0)))
    w_uv_heads = bf(jnp.transpose(w_uv, (1, 0, 2)))
    w_dq = bf(b_w_dq)
    w_o = bf(b_w_o)
    cache_krt = jnp.transpose(cache_k_rope, (0, 2, 1))
    w_uq4 = b_w_uq.reshape(N_B_LAYERS, Q_LORA, N_HEADS, QK_DIM)
    w_uq_rope = w_uq4[..., QK_NOPE:]
    w_uq = bf(jnp.concatenate(
        [w_uq4[..., :QK_NOPE].reshape(N_B_LAYERS, Q_LORA, N_HEADS * QK_NOPE),
         w_uq_rope.reshape(N_B_LAYERS, Q_LORA, N_HEADS * QK_ROPE),
         swap_halves(w_uq_rope).reshape(N_B_LAYERS, Q_LORA, N_HEADS * QK_ROPE)], axis=-1))

    bias_chunk = jnp.repeat(jnp.transpose(a_b_s, (0, 2, 1)), GROUP_DIM, axis=2)
    bias_single = jnp.repeat(a_b_s[:, :, 0], GROUP_DIM, axis=1)[:, None, :]
    w_single = jnp.repeat(a_w_s[:, :, 0, 0], GROUP_DIM, axis=1)[:, None, :]

    def mixer(x, l, *, tm, single):
        return _mixer_call(
            x, _row(norm_pre_mix[l]), (w_in, l), _row(a_ln_g[l]), _row(a_ln_b[l]),
            (w_single if single else a_w_s, l), (bias_single if single else bias_chunk, l),
            (w_out, l), _row(norm_post_mix[l]), tm=tm, single=single)

    def ffn_weights(l):
        return _row(norm_pre_ffn[l]), (w_up, l), (w_down, l), _row(norm_post_ffn[l])

    def tail(x, l, attn, *, tm):
        if attn is None:
            return _tail_call(x, None, None, None, *ffn_weights(l), tm=tm)
        return _tail_call(x, attn, (w_o, l - N_A_LAYERS), _row(norm_post_mix[l]),
                          *ffn_weights(l), tm=tm)

    def queries(x, l, tables, *, tm, seq_len, absorb):
        j = l - N_A_LAYERS
        return _q_call(x, _row(norm_pre_mix[l]), (w_dq, j), _row(b_q_norm[j]), (w_uq, j),
                       tables[2], tables[3], w_ukt, tm=tm, seq=seq_len, absorb=absorb)

    def latent(x, tables, *, tm, seq_len, expand):
        return _latent_call(x, _row(kv_norm_in), w_dkv_b, _row(kv_latent_norm), tables[0],
                            tables[1], w_uk2d, w_uv2d, tm=tm, seq=seq_len, expand=expand)

    assert N_B_LAYERS <= N_A_LAYERS
    ts = db * dseq
    tp = PROMPT_TOKEN_TILE
    tables_s = _rope_tables(jnp.full((ts,), n_past, dtype=jnp.int32), N_HEADS)
    tables_p = _rope_tables(jnp.arange(seq, dtype=jnp.int32), N_HEADS)

    xs = x_sample.reshape(ts, D_MODEL)
    v_rows = []
    for l in range(N_A_LAYERS):
        xs, v = mixer(xs, l, tm=ts, single=True)
        v_rows.append(v)
        xs = tail(xs, l, None, tm=ts)
    c_s, kr_s = latent(xs, tables_s, tm=ts, seq_len=ts, expand=False)
    q_s = queries(xs, N_A_LAYERS, tables_s, tm=ts, seq_len=ts, absorb=True)

    xp = x_prompt.reshape(nb * seq, D_MODEL)
    for l in range(N_A_LAYERS):
        (xp,) = mixer(xp, l, tm=tp, single=False)
        if l >= N_B_LAYERS:
            xp = tail(xp, l, None, tm=tp)
            continue
        lb = N_A_LAYERS + l
        xp, o_lat = _ffn_sample_attn_call(
            xp, *ffn_weights(l), page_table, jnp.transpose(q_s, (1, 0, 2)), c_s, kr_s,
            cache_kv_latent, cache_krt, ch=PAGES_PER_CHUNK, depth=PAGE_RING_DEPTH,
            rows=SAMPLE_ROWS_PER_STEP)
        attn_s = _uv_call(jnp.transpose(o_lat, (1, 0, 2)), w_uv_heads)
        xs = tail(xs, lb, attn_s, tm=ts)
        if l + 1 < N_B_LAYERS:
            q_s = queries(xs, lb + 1, tables_s, tm=ts, seq_len=ts, absorb=True)

    c_p, kr_p, kfull, vfull = latent(xp, tables_p, tm=tp, seq_len=seq, expand=True)
    for l in range(N_A_LAYERS, N_A_LAYERS + N_B_LAYERS):
        q_p = queries(xp, l, tables_p, tm=tp, seq_len=seq, absorb=False)
        attn_p = _prompt_attn_call(q_p, kfull, vfull, tq=ATTN_Q_TILE, tk=ATTN_KV_TILE,
                                   hb=ATTN_HEADS_PER_STEP)
        xp = tail(xp, l, attn_p.reshape(nb * seq, N_HEADS * V_HEAD), tm=tp)

    gate_v = jnp.stack(v_rows, axis=0).reshape(N_A_LAYERS, db, dseq, D_GATE)
    return (xp.reshape(nb, seq, D_MODEL), xs.reshape(db, dseq, D_MODEL),
            c_p.reshape(nb, seq, KV_LORA), kr_p.reshape(nb, seq, QK_ROPE),
            c_s.reshape(db, dseq, KV_LORA), kr_s.reshape(db, dseq, QK_ROPE), gate_v)


def kernel(x_prompt, x_sample, cache_kv_latent, cache_k_rope, page_table, norm_pre_mix, norm_post_mix, norm_pre_ffn, norm_post_ffn, a_w_in, a_ln_g, a_ln_b, a_w_s, a_b_s, a_w_out, kv_norm_in, w_dkv, kv_latent_norm, w_uk, w_uv, b_w_dq, b_q_norm, b_w_uq, b_w_o, ffn_w_up, ffn_w_down):
    assert x_sample.shape[1] == 1, "sample path assumes one new token per row"
    return _forward(x_prompt, x_sample, cache_kv_latent, cache_k_rope, page_table,
                    norm_pre_mix, norm_post_mix, norm_pre_ffn, norm_post_ffn,
                    a_w_in, a_ln_g, a_ln_b, a_w_s, a_b_s, a_w_out,
                    kv_norm_in, w_dkv, kv_latent_norm, w_uk, w_uv,
                    b_w_dq, b_q_norm, b_w_uq, b_w_o, ffn_w_up, ffn_w_down)
```
